```python
import math
import jax, jax.numpy as jnp
from jax import lax
import numpy as np

D_MODEL = 1024
BATCH = 2
SEQ = 8192
DEPTH = 2

CONV_WIDTH = D_MODEL
CONV_KERNEL = 31
N_HEADS = 16
HEAD_DIM = 64
ATTN_WIDTH = N_HEADS * HEAD_DIM
MOBA_BLOCK = 256
MOBA_TOP_K = 3
Q_CHUNK = 32
ROPE_THETA = 10000.0
LN_EPS = 1e-5
NEG_INF = -1e30
DEEPNORM_ALPHA = (2 * DEPTH) ** 0.25
DEEPNORM_BETA = (8 * DEPTH) ** -0.25
SPLIT_SIZES = (CONV_WIDTH, CONV_WIDTH, CONV_WIDTH,
               ATTN_WIDTH, ATTN_WIDTH, ATTN_WIDTH, ATTN_WIDTH,
               D_MODEL, D_MODEL)
N_IN = sum(SPLIT_SIZES)
SPLIT_POINTS = tuple(int(v) for v in np.cumsum(SPLIT_SIZES)[:-1])

kernel_name = "hybrid_conformer_conv_moba_gated_deepnorm"


def layer_norm(x, g, b):
    xf = x.astype(jnp.float32)
    mu = jnp.mean(xf, axis=-1, keepdims=True)
    var = jnp.mean(jnp.square(xf - mu), axis=-1, keepdims=True)
    y = (xf - mu) * lax.rsqrt(var + LN_EPS)
    return (y * g.astype(jnp.float32) + b.astype(jnp.float32)).astype(x.dtype)


def rope(x, positions):
    half = HEAD_DIM // 2
    inv_freq = ROPE_THETA ** (-jnp.arange(half, dtype=jnp.float32) / half)
    ang = positions.astype(jnp.float32)[:, None] * inv_freq[None, :]
    cos, sin = jnp.cos(ang), jnp.sin(ang)
    xf = x.astype(jnp.float32)
    x1, x2 = xf[..., :half], xf[..., half:]
    out = jnp.concatenate([x1 * cos - x2 * sin, x2 * cos + x1 * sin], axis=-1)
    return out.astype(x.dtype)


def conformer_conv(a_val, a_glu, conv_w, conv_b, cln_g, cln_b, w_pw2):
    h = a_val * jax.nn.sigmoid(a_glu)
    h = lax.conv_general_dilated(
        h, conv_w[:, None, :].astype(h.dtype), window_strides=(1,),
        padding=[(CONV_KERNEL - 1, 0)],
        dimension_numbers=('NWC', 'WIO', 'NWC'),
        feature_group_count=CONV_WIDTH) + conv_b
    h = jax.nn.silu(layer_norm(h, cln_g, cln_b))
    return h @ w_pw2


def moba_attention(q, k, v):
    B, H, S, Dh = q.shape
    nb = -(-S // MOBA_BLOCK)
    pad = nb * MOBA_BLOCK - S
    k_blk = jnp.pad(k, ((0, 0), (0, 0), (0, pad), (0, 0))).reshape(B, H, nb, MOBA_BLOCK, Dh)
    v_blk = jnp.pad(v, ((0, 0), (0, 0), (0, pad), (0, 0))).reshape(B, H, nb, MOBA_BLOCK, Dh)
    k_mean = jnp.mean(k_blk.astype(jnp.float32), axis=3)
    n_sel = min(MOBA_TOP_K, nb)
    scale = 1.0 / math.sqrt(Dh)
    nc = S // Q_CHUNK
    q_chunks = q.reshape(B, H, nc, Q_CHUNK, Dh).transpose(2, 0, 1, 3, 4)
    b_idx = jnp.arange(B)[:, None, None, None]
    h_idx = jnp.arange(H)[None, :, None, None]
    blk_ids = jnp.arange(nb)

    def chunk_fn(args):
        q_c, c = args
        q_pos = c * Q_CHUNK + jnp.arange(Q_CHUNK)
        own = (c * Q_CHUNK) // MOBA_BLOCK
        gate = jnp.einsum('bhqd,bhnd->bhqn', q_c.astype(jnp.float32), k_mean)
        past = blk_ids < own
        gate = jnp.where(past[None, None, None, :], gate, NEG_INF)
        _, top_i = lax.top_k(gate, n_sel)
        valid = past[top_i]
        k_sel = k_blk[b_idx, h_idx, top_i]
        v_sel = v_blk[b_idx, h_idx, top_i]
        qs = q_c * scale
        s_past = jnp.einsum('bhqd,bhqnkd->bhqnk', qs, k_sel).astype(jnp.float32)
        s_past = jnp.where(valid[..., None], s_past, NEG_INF)
        s_past = s_past.reshape(B, H, Q_CHUNK, n_sel * MOBA_BLOCK)
        k_own = lax.dynamic_index_in_dim(k_blk, own, axis=2, keepdims=False)
        v_own = lax.dynamic_index_in_dim(v_blk, own, axis=2, keepdims=False)
        key_pos = own * MOBA_BLOCK + jnp.arange(MOBA_BLOCK)
        causal = key_pos[None, :] <= q_pos[:, None]
        s_own = jnp.einsum('bhqd,bhkd->bhqk', qs, k_own).astype(jnp.float32)
        s_own = jnp.where(causal[None, None], s_own, NEG_INF)
        p = jax.nn.softmax(jnp.concatenate([s_past, s_own], axis=-1), axis=-1)
        p_past = p[..., :n_sel * MOBA_BLOCK].reshape(B, H, Q_CHUNK, n_sel, MOBA_BLOCK).astype(v.dtype)
        p_own = p[..., n_sel * MOBA_BLOCK:].astype(v.dtype)
        return (jnp.einsum('bhqnk,bhqnkd->bhqd', p_past, v_sel)
                + jnp.einsum('bhqk,bhkd->bhqd', p_own, v_own))

    out = lax.map(chunk_fn, (q_chunks, jnp.arange(nc)))
    return out.transpose(1, 2, 0, 3, 4).reshape(B, H, S, Dh)


def hybrid_layer(x, w_in, b_in, conv_w, conv_b, cln_g, cln_b, w_pw2,
                 w_proj_a, w_proj_b, w_out, ln_g, ln_b, positions):
    B, S, _ = x.shape
    u = x @ w_in + b_in
    a_val, a_glu, a_z, q, k, v, b_z, g_a, g_b = jnp.split(u, SPLIT_POINTS, axis=-1)
    h_a = conformer_conv(a_val, a_glu, conv_w, conv_b, cln_g, cln_b, w_pw2) * jax.nn.silu(a_z)
    y_a = h_a @ w_proj_a
    to_heads = lambda t: t.reshape(B, S, N_HEADS, HEAD_DIM).transpose(0, 2, 1, 3)
    qh = rope(to_heads(q), positions)
    kh = rope(to_heads(k), positions)
    o = moba_attention(qh, kh, to_heads(v))
    o = o.transpose(0, 2, 1, 3).reshape(B, S, ATTN_WIDTH)
    y_b = (o * jax.nn.silu(b_z)) @ w_proj_b
    merged = jax.nn.sigmoid(g_a) * y_a + jax.nn.sigmoid(g_b) * y_b
    out = merged @ w_out
    return layer_norm(DEEPNORM_ALPHA * x + out, ln_g, ln_b)


def setup_inputs(seed: int = 0) -> dict:
    key = jax.random.key(seed)
    ks = jax.random.split(key, 14)
    f32 = jnp.float32
    nrm = lambda k, shape: jax.random.normal(k, shape, dtype=f32)
    return {
        "x": nrm(ks[0], (BATCH, SEQ, D_MODEL)),
        "w_in": nrm(ks[1], (DEPTH, D_MODEL, N_IN)) * D_MODEL ** -0.5,
        "b_in": 0.01 * nrm(ks[2], (DEPTH, N_IN)),
        "conv_w": nrm(ks[3], (DEPTH, CONV_KERNEL, CONV_WIDTH)) * CONV_KERNEL ** -0.5,
        "conv_b": 0.01 * nrm(ks[4], (DEPTH, CONV_WIDTH)),
        "conv_ln_g": 1.0 + 0.01 * nrm(ks[5], (DEPTH, CONV_WIDTH)),
        "conv_ln_b": 0.01 * nrm(ks[6], (DEPTH, CONV_WIDTH)),
        "w_pw2": nrm(ks[7], (DEPTH, CONV_WIDTH, CONV_WIDTH)) * CONV_WIDTH ** -0.5,
        "w_proj_a": nrm(ks[8], (DEPTH, CONV_WIDTH, D_MODEL)) * CONV_WIDTH ** -0.5 * DEEPNORM_BETA,
        "w_proj_b": nrm(ks[9], (DEPTH, ATTN_WIDTH, D_MODEL)) * ATTN_WIDTH ** -0.5 * DEEPNORM_BETA,
        "w_out": nrm(ks[10], (DEPTH, D_MODEL, D_MODEL)) * D_MODEL ** -0.5 * DEEPNORM_BETA,
        "ln_g": 1.0 + 0.01 * nrm(ks[11], (DEPTH, D_MODEL)),
        "ln_b": 0.01 * nrm(ks[12], (DEPTH, D_MODEL)),
    }


def reference(x, w_in, b_in, conv_w, conv_b, conv_ln_g, conv_ln_b, w_pw2,
              w_proj_a, w_proj_b, w_out, ln_g, ln_b):
    positions = jnp.arange(x.shape[1], dtype=jnp.int32)
    for l in range(DEPTH):
        x = hybrid_layer(x, w_in[l], b_in[l], conv_w[l], conv_b[l], conv_ln_g[l], conv_ln_b[l],
                         w_pw2[l], w_proj_a[l], w_proj_b[l], w_out[l], ln_g[l], ln_b[l], positions)
    return x
```

```python
import functools
import math

import jax
import jax.numpy as jnp
from jax import lax
from jax.experimental import pallas as pl
from jax.experimental.pallas import tpu as pltpu

N_HEADS = 16
HEAD_DIM = 64
HALF = HEAD_DIM // 2
MOBA_BLOCK = 256
MOBA_TOP_K = 3
CONV_KERNEL = 31
ROPE_THETA = 10000.0
LN_EPS = 1e-5
NEG_INF = -1e30

LANES = 128
SUBLANES = 8
HEADS_PER_STEP = LANES // HEAD_DIM
CONV_HALO = 32
CONV_ROWS = 64
ONES_ROWS = 16
ROW_TILE = MOBA_BLOCK
VMEM_LIMIT = 56 * 1024 * 1024

BF16 = jnp.bfloat16
F32 = jnp.float32


def _dot(a, b):
    return jnp.dot(a, b, preferred_element_type=F32)


def _dot_nt(a, b):
    return lax.dot_general(a, b, (((1,), (1,)), ((), ())), preferred_element_type=F32)


def _sigmoid(v):
    return 1.0 / (1.0 + jnp.exp(-v))


def _silu(v):
    return v * _sigmoid(v)


def _layer_norm(v, g, b):
    mu = jnp.mean(v, axis=-1, keepdims=True)
    vc = v - mu
    var = jnp.mean(vc * vc, axis=-1, keepdims=True)
    return vc * lax.rsqrt(var + LN_EPS) * g + b


def _const_spec(shape):
    return pl.BlockSpec(shape, lambda *_: (0,) * len(shape))


def _conv_branch_kernel(x_ref, wval_ref, wglu_ref, wz_ref, wga_ref, bval_ref, bglu_ref, bz_ref, bga_ref,
                        cw_ref, cb_ref, cg_ref, cbeta_ref, wpw2_ref, wpa_ref, out_ref, hbuf, cbuf,
                        *, tiles_per_batch):
    tm = x_ref.shape[0]
    n_chunks = x_ref.shape[1] // LANES
    t = pl.program_id(0)

    @pl.when(t % tiles_per_batch == 0)
    def _():
        hbuf[:, 0:CONV_HALO, :] = jnp.zeros((n_chunks, CONV_HALO, LANES), F32)

    xb = x_ref[...].astype(BF16)
    a_val = _dot(xb, wval_ref[...]) + bval_ref[...]
    a_glu = _dot(xb, wglu_ref[...]) + bglu_ref[...]
    h = a_val * _sigmoid(a_glu)
    for c in range(n_chunks):
        hbuf[c, CONV_HALO:CONV_HALO + tm, :] = h[:, c * LANES:(c + 1) * LANES]

    first = CONV_HALO - (CONV_KERNEL - 1)
    for c in range(n_chunks):
        lanes = slice(c * LANES, (c + 1) * LANES)

        def row_body(r, carry, c=c, lanes=lanes):
            base = pl.multiple_of(r * CONV_ROWS, CONV_ROWS)
            acc = jnp.broadcast_to(cb_ref[:, lanes], (CONV_ROWS, LANES))
            for k in range(CONV_KERNEL):
                acc = acc + hbuf[c, pl.ds(base + (first + k), CONV_ROWS), :] * cw_ref[k:k + 1, lanes]
            cbuf[pl.ds(base, CONV_ROWS), lanes] = acc
            return carry

        lax.fori_loop(0, tm // CONV_ROWS, row_body, 0)

    for c in range(n_chunks):
        hbuf[c, 0:CONV_HALO, :] = hbuf[c, tm:tm + CONV_HALO, :]

    y = _silu(_layer_norm(cbuf[...], cg_ref[...], cbeta_ref[...]))
    y = _dot(y.astype(BF16), wpw2_ref[...])
    a_z = _dot(xb, wz_ref[...]) + bz_ref[...]
    y = y * _silu(a_z)
    y = _dot(y.astype(BF16), wpa_ref[...])
    g_a = _dot(xb, wga_ref[...]) + bga_ref[...]
    out_ref[...] = _sigmoid(g_a) * y


def _conv_branch(xf, p, seq_len):
    m, d = xf.shape
    tm = ROW_TILE
    w_spec = _const_spec((d, d))
    b_spec = _const_spec((1, d))
    return pl.pallas_call(
        functools.partial(_conv_branch_kernel, tiles_per_batch=seq_len // tm),
        grid=(m // tm,),
        in_specs=[pl.BlockSpec((tm, d), lambda t: (t, 0)),
                  w_spec, w_spec, w_spec, w_spec, b_spec, b_spec, b_spec, b_spec,
                  _const_spec(p["conv_w"].shape), b_spec, b_spec, b_spec, w_spec, w_spec],
        out_specs=pl.BlockSpec((tm, d), lambda t: (t, 0)),
        out_shape=jax.ShapeDtypeStruct((m, d), F32),
        scratch_shapes=[pltpu.VMEM((d // LANES, tm + CONV_HALO, LANES), F32),
                        pltpu.VMEM((tm, d), F32)],
        compiler_params=pltpu.CompilerParams(dimension_semantics=("arbitrary",),
                                             vmem_limit_bytes=VMEM_LIMIT),
        name="conv_branch",
    )(xf, p["w_val"], p["w_glu"], p["w_az"], p["w_ga"], p["b_val"], p["b_glu"], p["b_az"], p["b_ga"],
      p["conv_w"], p["conv_b"], p["cln_g"], p["cln_b"], p["w_pw2"], p["w_proj_a"])


def _swap_halves_rows(v):
    parts = []
    for h in range(v.shape[0] // HEAD_DIM):
        base = h * HEAD_DIM
        parts += [v[base + HALF:base + HEAD_DIM], v[base:base + HALF]]
    return jnp.concatenate(parts, axis=0)


def _attn_proj_kernel(x_ref, wk_ref, wkr_ref, bk_ref, bkr_ref, wqt_ref, bq_ref, wvt_ref, bv_ref,
                      cosk_ref, sink_ref, cosq_ref, sinq_ref, k_ref, km_ref, qt_ref, vt_ref,
                      ):
    d = x_ref.shape[1]
    xb = x_ref[...].astype(BF16)

    k = _dot(xb, wk_ref[...]) + bk_ref[...]
    kr = _dot(xb, wkr_ref[...]) + bkr_ref[...]
    cos_k = cosk_ref[...]
    sin_k = sink_ref[...]
    for c in range(d // LANES):
        lanes = slice(c * LANES, (c + 1) * LANES)
        k_rot = k[:, lanes] * cos_k + kr[:, lanes] * sin_k
        k_ref[:, lanes] = k_rot.astype(BF16)
        km_ref[:, lanes] = jnp.broadcast_to(jnp.mean(k_rot, axis=0, keepdims=True), (SUBLANES, LANES))

    qt = _dot_nt(wqt_ref[...], xb) + bq_ref[...]
    cos_q = cosq_ref[...]
    sin_q = sinq_ref[...]
    scale = 1.0 / math.sqrt(HEAD_DIM)
    for c in range(d // LANES):
        rows = slice(c * LANES, (c + 1) * LANES)
        q = qt[rows]
        q_rot = (q * cos_q + _swap_halves_rows(q) * sin_q) * scale
        qt_ref[rows, :] = q_rot.astype(BF16)

    vt = _dot_nt(wvt_ref[...], xb) + bv_ref[...]
    vt_ref[...] = vt.astype(BF16)


def _attn_proj(xf, p, tables, batch, seq_len):
    m, d = xf.shape
    tm = MOBA_BLOCK
    nb = seq_len // tm
    w_spec = _const_spec((d, d))
    row_spec = _const_spec((1, d))
    col_spec = _const_spec((d, 1))
    cos_k, sin_k, cos_q, sin_q = tables
    k, km, qt, vt = pl.pallas_call(
        _attn_proj_kernel,
        grid=(m // tm,),
        in_specs=[pl.BlockSpec((tm, d), lambda t: (t, 0)),
                  w_spec, w_spec, row_spec, row_spec, w_spec, col_spec, w_spec, col_spec,
                  pl.BlockSpec((tm, LANES), lambda t: (t % nb, 0)),
                  pl.BlockSpec((tm, LANES), lambda t: (t % nb, 0)),
                  pl.BlockSpec((LANES, tm), lambda t: (0, t % nb)),
                  pl.BlockSpec((LANES, tm), lambda t: (0, t % nb))],
        out_specs=[pl.BlockSpec((None, None, tm, d), lambda t: (t // nb, t % nb, 0, 0)),
                   pl.BlockSpec((None, None, SUBLANES, d), lambda t: (t // nb, t % nb, 0, 0)),
                   pl.BlockSpec((None, None, d, tm), lambda t: (t // nb, t % nb, 0, 0)),
                   pl.BlockSpec((None, None, d, tm), lambda t: (t // nb, t % nb, 0, 0))],
        out_shape=[jax.ShapeDtypeStruct((batch, nb, tm, d), BF16),
                   jax.ShapeDtypeStruct((batch, nb, SUBLANES, d), F32),
                   jax.ShapeDtypeStruct((batch, nb, d, tm), BF16),
                   jax.ShapeDtypeStruct((batch, nb, d, tm), BF16)],
        compiler_params=pltpu.CompilerParams(dimension_semantics=("arbitrary",),
                                             vmem_limit_bytes=VMEM_LIMIT),
        name="attn_proj",
    )(xf, p["w_k"], p["w_k_rot"], p["b_k"], p["b_k_rot"], p["w_q_t"], p["b_q"], p["w_v_t"], p["b_v"],
      cos_k, sin_k, cos_q, sin_q)
    return k, km[:, :, 0, :], qt, vt


def _select_bias(gate, own, nb):
    blk = lax.broadcasted_iota(jnp.int32, gate.shape, 0)
    past = blk < own
    g = jnp.where(past, gate, NEG_INF)
    sel = jnp.zeros(gate.shape, jnp.bool_)
    for _ in range(min(MOBA_TOP_K, nb)):
        best = jnp.max(g, axis=0, keepdims=True)
        first = jnp.min(jnp.where(g == best, blk, nb), axis=0, keepdims=True)
        pick = blk == first
        sel = jnp.logical_or(sel, pick)
        g = jnp.where(pick, -jnp.inf, g)
    return jnp.where(jnp.logical_and(sel, past), 0.0, NEG_INF)


def _moba_kernel(qt_ref, k_ref, vt_ref, km_ref, ot_ref):
    nb = k_ref.shape[0]
    tk = k_ref.shape[1]
    tq = qt_ref.shape[1]
    own = pl.program_id(2)

    q2 = qt_ref[...]
    km = km_ref[...]
    feat_row = lax.broadcasted_iota(jnp.int32, q2.shape, 0)
    feat_lane = lax.broadcasted_iota(jnp.int32, km.shape, 1)
    key_pos = lax.broadcasted_iota(jnp.int32, (tk, tq), 0)
    qry_pos = lax.broadcasted_iota(jnp.int32, (tk, tq), 1)
    blk_lane = lax.broadcasted_iota(jnp.int32, (tk, LANES), 1)
    ones = jnp.ones((ONES_ROWS, tk), BF16)
    pad_rows = jnp.zeros((LANES - nb, tq), BF16)

    for h in range(HEADS_PER_STEP):
        lo, hi = h * HEAD_DIM, (h + 1) * HEAD_DIM
        qh = jnp.where(jnp.logical_and(feat_row >= lo, feat_row < hi), q2, jnp.zeros_like(q2))
        kmh = jnp.where(jnp.logical_and(feat_lane >= lo, feat_lane < hi), km, 0.0).astype(BF16)
        gate = _dot(kmh, q2)
        bias = _select_bias(gate, own, nb).astype(BF16)
        q_aug = jnp.concatenate([qh, bias, pad_rows], axis=0)

        def v_aug(j, lo=lo, hi=hi):
            return jnp.concatenate([vt_ref[j, lo:hi, :], ones], axis=0)

        s = _dot(k_ref[own], qh)
        s = jnp.where(key_pos <= qry_pos, s, NEG_INF)
        m0 = jnp.max(s, axis=0, keepdims=True)
        p = jnp.exp(s - m0)
        acc0 = _dot(v_aug(own), p.astype(BF16))

        def past_block(j, carry):
            m, acc = carry
            onehot = jnp.where(blk_lane == j, 1.0, 0.0).astype(BF16)
            s = _dot(jnp.concatenate([k_ref[j], onehot], axis=1), q_aug)
            m_new = jnp.maximum(m, jnp.max(s, axis=0, keepdims=True))
            p = jnp.exp(s - m_new)
            acc = acc * jnp.exp(m - m_new) + _dot(v_aug(j), p.astype(BF16))
            return m_new, acc

        _, acc = lax.fori_loop(0, own, past_block, (m0, acc0))
        ot_ref[lo:hi, :] = acc[0:HEAD_DIM] / acc[HEAD_DIM:HEAD_DIM + 1]


def _moba_attention(k, km, qt, vt):
    batch, nb, tk, d = k.shape
    tq = qt.shape[3]
    return pl.pallas_call(
        _moba_kernel,
        grid=(batch, d // LANES, nb),
        in_specs=[pl.BlockSpec((None, None, LANES, tq), lambda b, hp, i: (b, i, hp, 0)),
                  pl.BlockSpec((None, nb, tk, LANES), lambda b, hp, i: (b, 0, 0, hp)),
                  pl.BlockSpec((None, nb, LANES, tk), lambda b, hp, i: (b, 0, hp, 0)),
                  pl.BlockSpec((None, nb, LANES), lambda b, hp, i: (b, 0, hp))],
        out_specs=pl.BlockSpec((None, None, LANES, tq), lambda b, hp, i: (b, i, hp, 0)),
        out_shape=jax.ShapeDtypeStruct((batch, nb, d, tq), F32),
        compiler_params=pltpu.CompilerParams(dimension_semantics=("arbitrary", "arbitrary", "arbitrary"),
                                             vmem_limit_bytes=VMEM_LIMIT),
        name="moba_attn",
    )(qt, k, vt, km)


def _merge_kernel(x_ref, ot_ref, ya_ref, wbz_ref, wgb_ref, bbz_ref, bgb_ref, wpb_ref, wout_ref,
                  g_ref, beta_ref, out_ref, *, alpha):
    x = x_ref[...]
    xb = x.astype(BF16)
    o = ot_ref[...].T
    b_z = _dot(xb, wbz_ref[...]) + bbz_ref[...]
    y_b = _dot((o * _silu(b_z)).astype(BF16), wpb_ref[...])
    g_b = _dot(xb, wgb_ref[...]) + bgb_ref[...]
    merged = ya_ref[...] + _sigmoid(g_b) * y_b
    out = _dot(merged.astype(BF16), wout_ref[...])
    out_ref[...] = _layer_norm(alpha * x + out, g_ref[...], beta_ref[...])


def _merge(xf, ot, ya, p, alpha):
    m, d = xf.shape
    batch, nb, _, tm = ot.shape
    w_spec = _const_spec((d, d))
    b_spec = _const_spec((1, d))
    return pl.pallas_call(
        functools.partial(_merge_kernel, alpha=alpha),
        grid=(m // tm,),
        in_specs=[pl.BlockSpec((tm, d), lambda t: (t, 0)),
                  pl.BlockSpec((None, None, d, tm), lambda t: (t // nb, t % nb, 0, 0)),
                  pl.BlockSpec((tm, d), lambda t: (t, 0)),
                  w_spec, w_spec, b_spec, b_spec, w_spec, w_spec, b_spec, b_spec],
        out_specs=pl.BlockSpec((tm, d), lambda t: (t, 0)),
        out_shape=jax.ShapeDtypeStruct((m, d), F32),
        compiler_params=pltpu.CompilerParams(dimension_semantics=("arbitrary",),
                                             vmem_limit_bytes=VMEM_LIMIT),
        name="merge",
    )(xf, ot, ya, p["w_bz"], p["w_gb"], p["b_bz"], p["b_gb"], p["w_proj_b"], p["w_out"],
      p["ln_g"], p["ln_b"])


def _rope_tables(seq_len):
    inv_freq = ROPE_THETA ** (-jnp.arange(HALF, dtype=F32) / HALF)
    ang = jnp.arange(seq_len, dtype=jnp.int32).astype(F32)[:, None] * inv_freq[None, :]
    cos, sin = jnp.cos(ang), jnp.sin(ang)
    cos_head = jnp.concatenate([cos, cos], axis=1)
    sin_head = jnp.concatenate([-sin, sin], axis=1)
    cos_k = jnp.tile(cos_head, (1, HEADS_PER_STEP))
    sin_k = jnp.tile(sin_head, (1, HEADS_PER_STEP))
    return cos_k, sin_k, cos_k.T, sin_k.T


def _layer_params(l, w_in, b_in, conv_w, conv_b, conv_ln_g, conv_ln_b, w_pw2, w_proj_a, w_proj_b, w_out,
                  ln_g, ln_b):
    d = w_in.shape[1]
    w = w_in[l].reshape(d, -1, d)
    b = b_in[l].reshape(-1, d)
    names = ("val", "glu", "az", "q", "k", "v", "bz", "ga", "gb")
    wg = {n: w[:, i, :] for i, n in enumerate(names)}
    bg = {n: b[i] for i, n in enumerate(names)}
    perm = (jnp.arange(d).reshape(N_HEADS, 2, HALF)[:, ::-1, :]).reshape(d)
    row = lambda v: v.reshape(1, d).astype(F32)
    return {
        "w_val": wg["val"].astype(BF16), "w_glu": wg["glu"].astype(BF16), "w_az": wg["az"].astype(BF16),
        "w_ga": wg["ga"].astype(BF16), "w_bz": wg["bz"].astype(BF16), "w_gb": wg["gb"].astype(BF16),
        "b_val": row(bg["val"]), "b_glu": row(bg["glu"]), "b_az": row(bg["az"]),
        "b_ga": row(bg["ga"]), "b_bz": row(bg["bz"]), "b_gb": row(bg["gb"]),
        "w_k": wg["k"].astype(BF16), "w_k_rot": wg["k"][:, perm].astype(BF16),
        "b_k": row(bg["k"]), "b_k_rot": row(bg["k"][perm]),
        "w_q_t": wg["q"].T.astype(BF16), "b_q": bg["q"].reshape(d, 1),
        "w_v_t": wg["v"].T.astype(BF16), "b_v": bg["v"].reshape(d, 1),
        "conv_w": jnp.pad(conv_w[l], ((0, 1), (0, 0))), "conv_b": row(conv_b[l]),
        "cln_g": row(conv_ln_g[l]), "cln_b": row(conv_ln_b[l]),
        "w_pw2": w_pw2[l].astype(BF16), "w_proj_a": w_proj_a[l].astype(BF16),
        "w_proj_b": w_proj_b[l].astype(BF16), "w_out": w_out[l].astype(BF16),
        "ln_g": row(ln_g[l]), "ln_b": row(ln_b[l]),
    }


def kernel(x, w_in, b_in, conv_w, conv_b, conv_ln_g, conv_ln_b, w_pw2, w_proj_a, w_proj_b, w_out, ln_g, ln_b):
    batch, seq_len, d = x.shape
    depth = w_in.shape[0]
    assert d == N_HEADS * HEAD_DIM and w_in.shape[2] == 9 * d
    assert seq_len % MOBA_BLOCK == 0 and seq_len // MOBA_BLOCK <= LANES
    assert conv_w.shape[1] == CONV_KERNEL
    alpha = (2 * depth) ** 0.25
    tables = _rope_tables(seq_len)
    xf = x.reshape(batch * seq_len, d)
    for l in range(depth):
        p = _layer_params(l, w_in, b_in, conv_w, conv_b, conv_ln_g, conv_ln_b, w_pw2, w_proj_a, w_proj_b,
                          w_out, ln_g, ln_b)
        ya = _conv_branch(xf, p, seq_len)
        k, km, qt, vt = _attn_proj(xf, p, tables, batch, seq_len)
        ot = _moba_attention(k, km, qt, vt)
        xf = _merge(xf, ot, ya, p, alpha)
    return xf.reshape(batch, seq_len, d)
```

```python
import functools
import math

import jax
import jax.numpy as jnp
from jax import lax
from jax.experimental import pallas as pl
from jax.experimental.pallas import tpu as pltpu

N_HEADS = 16
HEAD_DIM = 64
HALF = HEAD_DIM // 2
MOBA_BLOCK = 256
MOBA_TOP_K = 3
CONV_KERNEL = 31
ROPE_THETA = 10000.0
LN_EPS = 1e-5
NEG_INF = -1e30
LOG2_E = math.log2(math.e)

LANES = 128
SUBLANES = 8
HEADS_PER_STEP = LANES // HEAD_DIM
CONV_HALO = 32
CONV_ROWS = 64
ATTN_HEADS_PER_STEP = 8
ONES_ROWS = 16
ROW_TILE = MOBA_BLOCK
VMEM_LIMIT = 56 * 1024 * 1024

BF16 = jnp.bfloat16
F32 = jnp.float32


def _dot(a, b):
    return jnp.dot(a, b, preferred_element_type=F32)


def _dot_nt(a, b):
    return lax.dot_general(a, b, (((1,), (1,)), ((), ())), preferred_element_type=F32)


def _sigmoid(v):
    return 1.0 / (1.0 + jnp.exp(-v))


def _silu(v):
    return v * _sigmoid(v)


def _layer_norm(v, g, b):
    mu = jnp.mean(v, axis=-1, keepdims=True)
    vc = v - mu
    var = jnp.mean(vc * vc, axis=-1, keepdims=True)
    return vc * lax.rsqrt(var + LN_EPS) * g + b


def _const_spec(shape):
    return pl.BlockSpec(shape, lambda *_: (0,) * len(shape))


def _conv_branch_kernel(x_ref, wval_ref, wglu_ref, wz_ref, wga_ref, bval_ref, bglu_ref, bz_ref, bga_ref,
                        cw_ref, cb_ref, cg_ref, cbeta_ref, wpw2_ref, wpa_ref, out_ref, hbuf, cbuf,
                        *, tiles_per_batch):
    tm = x_ref.shape[0]
    n_chunks = x_ref.shape[1] // LANES
    t = pl.program_id(0)

    @pl.when(t % tiles_per_batch == 0)
    def _():
        hbuf[:, 0:CONV_HALO, :] = jnp.zeros((n_chunks, CONV_HALO, LANES), F32)

    xb = x_ref[...].astype(BF16)
    a_val = _dot(xb, wval_ref[...]) + bval_ref[...]
    a_glu = _dot(xb, wglu_ref[...]) + bglu_ref[...]
    h = a_val * _sigmoid(a_glu)
    for c in range(n_chunks):
        hbuf[c, CONV_HALO:CONV_HALO + tm, :] = h[:, c * LANES:(c + 1) * LANES]

    first = CONV_HALO - (CONV_KERNEL - 1)
    for c in range(n_chunks):
        lanes = slice(c * LANES, (c + 1) * LANES)

        def row_body(r, carry, c=c, lanes=lanes):
            base = pl.multiple_of(r * CONV_ROWS, CONV_ROWS)
            acc = jnp.broadcast_to(cb_ref[:, lanes], (CONV_ROWS, LANES))
            for k in range(CONV_KERNEL):
                acc = acc + hbuf[c, pl.ds(base + (first + k), CONV_ROWS), :] * cw_ref[k:k + 1, lanes]
            cbuf[pl.ds(base, CONV_ROWS), lanes] = acc
            return carry

        lax.fori_loop(0, tm // CONV_ROWS, row_body, 0)

    for c in range(n_chunks):
        hbuf[c, 0:CONV_HALO, :] = hbuf[c, tm:tm + CONV_HALO, :]

    y = _silu(_layer_norm(cbuf[...], cg_ref[...], cbeta_ref[...]))
    y = _dot(y.astype(BF16), wpw2_ref[...])
    a_z = _dot(xb, wz_ref[...]) + bz_ref[...]
    y = y * _silu(a_z)
    y = _dot(y.astype(BF16), wpa_ref[...])
    g_a = _dot(xb, wga_ref[...]) + bga_ref[...]
    out_ref[...] = _sigmoid(g_a) * y


def _conv_branch(xf, p, seq_len):
    m, d = xf.shape
    tm = ROW_TILE
    w_spec = _const_spec((d, d))
    b_spec = _const_spec((1, d))
    return pl.pallas_call(
        functools.partial(_conv_branch_kernel, tiles_per_batch=seq_len // tm),
        grid=(m // tm,),
        in_specs=[pl.BlockSpec((tm, d), lambda t: (t, 0)),
                  w_spec, w_spec, w_spec, w_spec, b_spec, b_spec, b_spec, b_spec,
                  _const_spec(p["conv_w"].shape), b_spec, b_spec, b_spec, w_spec, w_spec],
        out_specs=pl.BlockSpec((tm, d), lambda t: (t, 0)),
        out_shape=jax.ShapeDtypeStruct((m, d), F32),
        scratch_shapes=[pltpu.VMEM((d // LANES, tm + CONV_HALO, LANES), F32),
                        pltpu.VMEM((tm, d), F32)],
        compiler_params=pltpu.CompilerParams(dimension_semantics=("arbitrary",),
                                             vmem_limit_bytes=VMEM_LIMIT),
        name="conv_branch",
    )(xf, p["w_val"], p["w_glu"], p["w_az"], p["w_ga"], p["b_val"], p["b_glu"], p["b_az"], p["b_ga"],
      p["conv_w"], p["conv_b"], p["cln_g"], p["cln_b"], p["w_pw2"], p["w_proj_a"])


def _swap_halves_rows(v):
    parts = []
    for h in range(v.shape[0] // HEAD_DIM):
        base = h * HEAD_DIM
        parts += [v[base + HALF:base + HEAD_DIM], v[base:base + HALF]]
    return jnp.concatenate(parts, axis=0)


def _attn_proj_kernel(x_ref, wk_ref, wkr_ref, bk_ref, bkr_ref, wqt_ref, bq_ref, wvt_ref, bv_ref,
                      cosk_ref, sink_ref, cosq_ref, sinq_ref, k_ref, km_ref, qt_ref, vt_ref,
                      ):
    d = x_ref.shape[1]
    xb = x_ref[...].astype(BF16)

    k = _dot(xb, wk_ref[...]) + bk_ref[...]
    kr = _dot(xb, wkr_ref[...]) + bkr_ref[...]
    cos_k = cosk_ref[...]
    sin_k = sink_ref[...]
    for c in range(d // LANES):
        lanes = slice(c * LANES, (c + 1) * LANES)
        k_rot = k[:, lanes] * cos_k + kr[:, lanes] * sin_k
        k_ref[:, lanes] = k_rot.astype(BF16)
        km_ref[:, lanes] = jnp.broadcast_to(jnp.mean(k_rot, axis=0, keepdims=True), (SUBLANES, LANES))

    qt = _dot_nt(wqt_ref[...], xb) + bq_ref[...]
    cos_q = cosq_ref[...]
    sin_q = sinq_ref[...]
    scale = LOG2_E / math.sqrt(HEAD_DIM)
    for c in range(d // LANES):
        rows = slice(c * LANES, (c + 1) * LANES)
        q = qt[rows]
        q_rot = (q * cos_q + _swap_halves_rows(q) * sin_q) * scale
        qt_ref[rows, :] = q_rot.astype(BF16)

    vt = _dot_nt(wvt_ref[...], xb) + bv_ref[...]
    vt_ref[...] = vt.astype(BF16)


def _attn_proj(xf, p, tables, batch, seq_len):
    m, d = xf.shape
    tm = MOBA_BLOCK
    nb = seq_len // tm
    w_spec = _const_spec((d, d))
    row_spec = _const_spec((1, d))
    col_spec = _const_spec((d, 1))
    cos_k, sin_k, cos_q, sin_q = tables
    k, km, qt, vt = pl.pallas_call(
        _attn_proj_kernel,
        grid=(m // tm,),
        in_specs=[pl.BlockSpec((tm, d), lambda t: (t, 0)),
                  w_spec, w_spec, row_spec, row_spec, w_spec, col_spec, w_spec, col_spec,
                  pl.BlockSpec((tm, LANES), lambda t: (t % nb, 0)),
                  pl.BlockSpec((tm, LANES), lambda t: (t % nb, 0)),
                  pl.BlockSpec((LANES, tm), lambda t: (0, t % nb)),
                  pl.BlockSpec((LANES, tm), lambda t: (0, t % nb))],
        out_specs=[pl.BlockSpec((None, None, tm, d), lambda t: (t // nb, t % nb, 0, 0)),
                   pl.BlockSpec((None, None, SUBLANES, d), lambda t: (t // nb, t % nb, 0, 0)),
                   pl.BlockSpec((None, None, d, tm), lambda t: (t // nb, t % nb, 0, 0)),
                   pl.BlockSpec((None, None, d, tm), lambda t: (t // nb, t % nb, 0, 0))],
        out_shape=[jax.ShapeDtypeStruct((batch, nb, tm, d), BF16),
                   jax.ShapeDtypeStruct((batch, nb, SUBLANES, d), F32),
                   jax.ShapeDtypeStruct((batch, nb, d, tm), BF16),
                   jax.ShapeDtypeStruct((batch, nb, d, tm), BF16)],
        compiler_params=pltpu.CompilerParams(dimension_semantics=("arbitrary",),
                                             vmem_limit_bytes=VMEM_LIMIT),
        name="attn_proj",
    )(xf, p["w_k"], p["w_k_rot"], p["b_k"], p["b_k_rot"], p["w_q_t"], p["b_q"], p["w_v_t"], p["b_v"],
      cos_k, sin_k, cos_q, sin_q)
    return k, km[:, :, 0, :], qt, vt


def _select_bias(gate, own, nb):
    blk = lax.broadcasted_iota(jnp.int32, gate.shape, 0)
    past = blk < own
    g = jnp.where(past, gate, NEG_INF)
    sel = jnp.zeros(gate.shape, jnp.bool_)
    for _ in range(min(MOBA_TOP_K, nb)):
        best = jnp.max(g, axis=0, keepdims=True)
        first = jnp.min(jnp.where(g == best, blk, nb), axis=0, keepdims=True)
        pick = blk == first
        sel = jnp.logical_or(sel, pick)
        g = jnp.where(pick, -jnp.inf, g)
    return jnp.where(jnp.logical_and(sel, past), 0.0, NEG_INF)


def _moba_kernel(qt_ref, k_ref, vt_ref, km_ref, ot_ref, qaug_scr, m_scr, acc_scr, s_even, s_odd):
    nb = k_ref.shape[0]
    tk = k_ref.shape[1]
    tq = qt_ref.shape[1]
    n_heads = qt_ref.shape[0] // HEAD_DIM
    own = pl.program_id(2)

    feat_row = lax.broadcasted_iota(jnp.int32, (LANES, tq), 0)
    feat_lane = lax.broadcasted_iota(jnp.int32, (nb, LANES), 1)
    key_pos = lax.broadcasted_iota(jnp.int32, (tk, tq), 0)
    qry_pos = lax.broadcasted_iota(jnp.int32, (tk, tq), 1)
    blk_lane = lax.broadcasted_iota(jnp.int32, (tk, LANES), 1)
    ones = jnp.ones((ONES_ROWS, tk), BF16)
    pad_rows = jnp.zeros((LANES - nb, tq), BF16)

    def pair_of(h):
        return slice((h // HEADS_PER_STEP) * LANES, (h // HEADS_PER_STEP + 1) * LANES)

    def v_aug(j, h):
        return jnp.concatenate([vt_ref[j, h * HEAD_DIM:(h + 1) * HEAD_DIM, :], ones], axis=0)

    def past_scores(j, h):
        onehot = jnp.where(blk_lane == j, 1.0, 0.0).astype(BF16)
        return _dot(jnp.concatenate([k_ref[j, :, pair_of(h)], onehot], axis=1), qaug_scr[h])

    def absorb(j, h, s):
        m = m_scr[h]
        m_new = jnp.maximum(m, jnp.max(s, axis=0, keepdims=True))
        p = jnp.exp2(s - m_new[0:1])
        acc_scr[h] = acc_scr[h] * jnp.exp2(m - m_new)[0:1] + _dot(v_aug(j, h), p.astype(BF16))
        m_scr[h] = m_new

    for h in range(n_heads):
        lo, hi = (h % HEADS_PER_STEP) * HEAD_DIM, (h % HEADS_PER_STEP + 1) * HEAD_DIM
        q2 = qt_ref[pair_of(h), :]
        km = km_ref[:, pair_of(h)]
        qh = jnp.where(jnp.logical_and(feat_row >= lo, feat_row < hi), q2, jnp.zeros_like(q2))
        kmh = jnp.where(jnp.logical_and(feat_lane >= lo, feat_lane < hi), km, 0.0).astype(BF16)
        gate = _dot(kmh, q2)
        bias = _select_bias(gate, own, nb).astype(BF16)
        qaug_scr[h] = jnp.concatenate([qh, bias, pad_rows], axis=0)
        s_odd[h] = _dot(k_ref[own, :, pair_of(h)], qh)

    for h in range(n_heads):
        s_even[h] = past_scores(0, h)
        s = jnp.where(key_pos <= qry_pos, s_odd[h], NEG_INF)
        m0 = jnp.max(s, axis=0, keepdims=True)
        p = jnp.exp2(s - m0)
        acc_scr[h] = _dot(v_aug(own, h), p.astype(BF16))
        m_scr[h] = jnp.broadcast_to(m0, (SUBLANES, tq))

    def two_blocks(t, carry):
        j0 = 2 * t
        j1 = jnp.minimum(j0 + 1, nb - 1)
        j2 = jnp.minimum(j0 + 2, nb - 1)
        for h in range(n_heads):
            s_odd[h] = past_scores(j1, h)
            absorb(j0, h, s_even[h])
        for h in range(n_heads):
            s_even[h] = past_scores(j2, h)
            absorb(j1, h, s_odd[h])
        return carry

    lax.fori_loop(0, (own + 1) // 2, two_blocks, 0)
    for h in range(n_heads):
        ot_ref[h * HEAD_DIM:(h + 1) * HEAD_DIM, :] = (acc_scr[h, 0:HEAD_DIM, :]
                                                       / acc_scr[h, HEAD_DIM:HEAD_DIM + 1, :])


def _moba_attention(k, km, qt, vt):
    batch, nb, tk, d = k.shape
    tq = qt.shape[3]
    width = ATTN_HEADS_PER_STEP * HEAD_DIM
    resident = pl.Buffered(1)
    return pl.pallas_call(
        _moba_kernel,
        grid=(batch, d // width, nb),
        in_specs=[pl.BlockSpec((None, None, width, tq), lambda b, g, i: (b, i, g, 0)),
                  pl.BlockSpec((None, nb, tk, width), lambda b, g, i: (b, 0, 0, g), pipeline_mode=resident),
                  pl.BlockSpec((None, nb, width, tk), lambda b, g, i: (b, 0, g, 0), pipeline_mode=resident),
                  pl.BlockSpec((None, nb, width), lambda b, g, i: (b, 0, g))],
        out_specs=pl.BlockSpec((None, None, width, tq), lambda b, g, i: (b, i, g, 0)),
        out_shape=jax.ShapeDtypeStruct((batch, nb, d, tq), F32),
        scratch_shapes=[pltpu.VMEM((ATTN_HEADS_PER_STEP, 2 * LANES, tq), BF16),
                        pltpu.VMEM((ATTN_HEADS_PER_STEP, SUBLANES, tq), F32),
                        pltpu.VMEM((ATTN_HEADS_PER_STEP, HEAD_DIM + ONES_ROWS, tq), F32),
                        pltpu.VMEM((ATTN_HEADS_PER_STEP, tk, tq), F32),
                        pltpu.VMEM((ATTN_HEADS_PER_STEP, tk, tq), F32)],
        compiler_params=pltpu.CompilerParams(dimension_semantics=("arbitrary", "arbitrary", "arbitrary"),
                                             vmem_limit_bytes=VMEM_LIMIT),
        name="moba_attn",
    )(qt, k, vt, km)


def _merge_kernel(x_ref, ot_ref, ya_ref, wbz_ref, wgb_ref, bbz_ref, bgb_ref, wpb_ref, wout_ref,
                  g_ref, beta_ref, out_ref, *, alpha):
    x = x_ref[...]
    xb = x.astype(BF16)
    o = ot_ref[...].T
    b_z = _dot(xb, wbz_ref[...]) + bbz_ref[...]
    y_b = _dot((o * _silu(b_z)).astype(BF16), wpb_ref[...])
    g_b = _dot(xb, wgb_ref[...]) + bgb_ref[...]
    merged = ya_ref[...] + _sigmoid(g_b) * y_b
    out = _dot(merged.astype(BF16), wout_ref[...])
    out_ref[...] = _layer_norm(alpha * x + out, g_ref[...], beta_ref[...])


def _merge(xf, ot, ya, p, alpha):
    m, d = xf.shape
    batch, nb, _, tm = ot.shape
    w_spec = _const_spec((d, d))
    b_spec = _const_spec((1, d))
    return pl.pallas_call(
        functools.partial(_merge_kernel, alpha=alpha),
        grid=(m // tm,),
        in_specs=[pl.BlockSpec((tm, d), lambda t: (t, 0)),
                  pl.BlockSpec((None, None, d, tm), lambda t: (t // nb, t % nb, 0, 0)),
                  pl.BlockSpec((tm, d), lambda t: (t, 0)),
                  w_spec, w_spec, b_spec, b_spec, w_spec, w_spec, b_spec, b_spec],
        out_specs=pl.BlockSpec((tm, d), lambda t: (t, 0)),
        out_shape=jax.ShapeDtypeStruct((m, d), F32),
        compiler_params=pltpu.CompilerParams(dimension_semantics=("arbitrary",),
                                             vmem_limit_bytes=VMEM_LIMIT),
        name="merge",
    )(xf, ot, ya, p["w_bz"], p["w_gb"], p["b_bz"], p["b_gb"], p["w_proj_b"], p["w_out"],
      p["ln_g"], p["ln_b"])


def _rope_tables(seq_len):
    inv_freq = ROPE_THETA ** (-jnp.arange(HALF, dtype=F32) / HALF)
    ang = jnp.arange(seq_len, dtype=jnp.int32).astype(F32)[:, None] * inv_freq[None, :]
    cos, sin = jnp.cos(ang), jnp.sin(ang)
    cos_head = jnp.concatenate([cos, cos], axis=1)
    sin_head = jnp.concatenate([-sin, sin], axis=1)
    cos_k = jnp.tile(cos_head, (1, HEADS_PER_STEP))
    sin_k = jnp.tile(sin_head, (1, HEADS_PER_STEP))
    return cos_k, sin_k, cos_k.T, sin_k.T


def _layer_params(l, w_in, b_in, conv_w, conv_b, conv_ln_g, conv_ln_b, w_pw2, w_proj_a, w_proj_b, w_out,
                  ln_g, ln_b):
    d = w_in.shape[1]
    w = w_in[l].reshape(d, -1, d)
    b = b_in[l].reshape(-1, d)
    names = ("val", "glu", "az", "q", "k", "v", "bz", "ga", "gb")
    wg = {n: w[:, i, :] for i, n in enumerate(names)}
    bg = {n: b[i] for i, n in enumerate(names)}
    perm = (jnp.arange(d).reshape(N_HEADS, 2, HALF)[:, ::-1, :]).reshape(d)
    row = lambda v: v.reshape(1, d).astype(F32)
    return {
        "w_val": wg["val"].astype(BF16), "w_glu": wg["glu"].astype(BF16), "w_az": wg["az"].astype(BF16),
        "w_ga": wg["ga"].astype(BF16), "w_bz": wg["bz"].astype(BF16), "w_gb": wg["gb"].astype(BF16),
        "b_val": row(bg["val"]), "b_glu": row(bg["glu"]), "b_az": row(bg["az"]),
        "b_ga": row(bg["ga"]), "b_bz": row(bg["bz"]), "b_gb": row(bg["gb"]),
        "w_k": wg["k"].astype(BF16), "w_k_rot": wg["k"][:, perm].astype(BF16),
        "b_k": row(bg["k"]), "b_k_rot": row(bg["k"][perm]),
        "w_q_t": wg["q"].T.astype(BF16), "b_q": bg["q"].reshape(d, 1),
        "w_v_t": wg["v"].T.astype(BF16), "b_v": bg["v"].reshape(d, 1),
        "conv_w": jnp.pad(conv_w[l], ((0, 1), (0, 0))), "conv_b": row(conv_b[l]),
        "cln_g": row(conv_ln_g[l]), "cln_b": row(conv_ln_b[l]),
        "w_pw2": w_pw2[l].astype(BF16), "w_proj_a": w_proj_a[l].astype(BF16),
        "w_proj_b": w_proj_b[l].astype(BF16), "w_out": w_out[l].astype(BF16),
        "ln_g": row(ln_g[l]), "ln_b": row(ln_b[l]),
    }


def kernel(x, w_in, b_in, conv_w, conv_b, conv_ln_g, conv_ln_b, w_pw2, w_proj_a, w_proj_b, w_out, ln_g, ln_b):
    batch, seq_len, d = x.shape
    depth = w_in.shape[0]
    assert d == N_HEADS * HEAD_DIM and w_in.shape[2] == 9 * d
    assert seq_len % MOBA_BLOCK == 0 and seq_len // MOBA_BLOCK <= LANES
    assert conv_w.shape[1] == CONV_KERNEL
    alpha = (2 * depth) ** 0.25
    tables = _rope_tables(seq_len)
    xf = x.reshape(batch * seq_len, d)
    for l in range(depth):
        p = _layer_params(l, w_in, b_in, conv_w, conv_b, conv_ln_g, conv_ln_b, w_pw2, w_proj_a, w_proj_b,
                          w_out, ln_g, ln_b)
        ya = _conv_branch(xf, p, seq_len)
        k, km, qt, vt = _attn_proj(xf, p, tables, batch, seq_len)
        ot = _moba_attention(k, km, qt, vt)
        xf = _merge(xf, ot, ya, p, alpha)
    return xf.reshape(batch, seq_len, d)
```

```python
import functools
import math

import jax
import jax.numpy as jnp
import numpy as np
from jax import lax
from jax.experimental import pallas as pl
from jax.experimental.pallas import tpu as pltpu

N_HEADS = 16
HEAD_DIM = 64
HALF = HEAD_DIM // 2
MOBA_BLOCK = 256
MOBA_TOP_K = 3
CONV_KERNEL = 31
ROPE_THETA = 10000.0
LN_EPS = 1e-5
NEG_INF = -1e30
LOG2_E = math.log2(math.e)

LANES = 128
SUBLANES = 8
HEADS_PER_STEP = LANES // HEAD_DIM
CONV_HALO = 32
CONV_ROWS = 64
CONV_SUB_ROWS = 256
CONV_ROW_TILE = 512
MXU_COLS = 256
ATTN_HEADS_PER_STEP = 8
ONES_ROWS = 16
ROW_TILE = MOBA_BLOCK
VMEM_LIMIT = 56 * 1024 * 1024

BF16 = jnp.bfloat16
F32 = jnp.float32


def _dot(a, b):
    return jnp.dot(a, b, preferred_element_type=F32)


def _dot_nt(a, b):
    return lax.dot_general(a, b, (((1,), (1,)), ((), ())), preferred_element_type=F32)


def _sigmoid(v):
    return 1.0 / (1.0 + jnp.exp(-v))


def _silu(v):
    return v * _sigmoid(v)


def _layer_norm(v, g, b):
    mu = jnp.mean(v, axis=-1, keepdims=True)
    vc = v - mu
    var = jnp.mean(vc * vc, axis=-1, keepdims=True)
    return vc * lax.rsqrt(var + LN_EPS) * g + b


def _const_spec(shape):
    return pl.BlockSpec(shape, lambda *_: (0,) * len(shape), pipeline_mode=pl.Buffered(1))


def _conv_branch_kernel(x_ref, wval_ref, wglu_ref, wz_ref, wga_ref, bval_ref, bglu_ref, bz_ref, bga_ref,
                        cw_ref, cb_ref, cg_ref, cbeta_ref, wpw2_ref, wpa_ref, out_ref,
                        hbuf, cbuf, za_buf, ga_buf, y2_buf, *, tiles_per_batch):
    tm, d = x_ref.shape
    n_sub = tm // CONV_SUB_ROWS
    n_chunks = d // LANES
    n_cols = d // MXU_COLS
    assert n_chunks == 2 * n_cols
    t = pl.program_id(0)

    @pl.when(t % tiles_per_batch == 0)
    def _():
        hbuf[:, 0:CONV_HALO, :] = jnp.zeros((n_chunks, CONV_HALO, LANES), F32)

    xb = [x_ref[u * CONV_SUB_ROWS:(u + 1) * CONV_SUB_ROWS, :].astype(BF16) for u in range(n_sub)]

    def glu_to_hbuf(u):
        for n in range(n_cols):
            cols = slice(n * MXU_COLS, (n + 1) * MXU_COLS)
            a_val = _dot(xb[u], wval_ref[n]) + bval_ref[:, cols]
            a_glu = _dot(xb[u], wglu_ref[n]) + bglu_ref[:, cols]
            h = a_val * _sigmoid(a_glu)
            for c2 in range(MXU_COLS // LANES):
                rows = slice(CONV_HALO + u * CONV_SUB_ROWS, CONV_HALO + (u + 1) * CONV_SUB_ROWS)
                hbuf[n * (MXU_COLS // LANES) + c2, rows, :] = h[:, c2 * LANES:(c2 + 1) * LANES]

    def conv_chunk(u, c):
        first = CONV_HALO - (CONV_KERNEL - 1)
        lanes = slice(c * LANES, (c + 1) * LANES)
        for r in range(CONV_SUB_ROWS // CONV_ROWS):
            base = u * CONV_SUB_ROWS + r * CONV_ROWS + first
            acc = jnp.broadcast_to(cb_ref[:, lanes], (CONV_ROWS, LANES))
            for k in range(CONV_KERNEL):
                acc = acc + hbuf[c, base + k:base + k + CONV_ROWS, :] * cw_ref[k:k + 1, lanes]
            cbuf[u, r * CONV_ROWS:(r + 1) * CONV_ROWS, lanes] = acc

    def gate_piece(u, n):
        if n < n_cols:
            cols = slice(n * MXU_COLS, (n + 1) * MXU_COLS)
            za_buf[u, :, cols] = _silu(_dot(xb[u], wz_ref[n % n_cols]) + bz_ref[:, cols])
        else:
            cols = slice((n - n_cols) * MXU_COLS, (n - n_cols + 1) * MXU_COLS)
            ga_buf[u, :, cols] = _sigmoid(_dot(xb[u], wga_ref[n % n_cols]) + bga_ref[:, cols])

    def normed(u):
        return _silu(_layer_norm(cbuf[u], cg_ref[...], cbeta_ref[...])).astype(BF16)

    def tail_piece(u, y, n):
        if n < n_cols:
            cols = slice(n * MXU_COLS, (n + 1) * MXU_COLS)
            y2_buf[u, :, cols] = (_dot(y, wpw2_ref[n % n_cols]) * za_buf[u, :, cols]).astype(BF16)
        else:
            cols = slice((n - n_cols) * MXU_COLS, (n - n_cols + 1) * MXU_COLS)
            rows = slice(u * CONV_SUB_ROWS, (u + 1) * CONV_SUB_ROWS)
            out_ref[rows, cols] = _dot(y2_buf[u], wpa_ref[n % n_cols]) * ga_buf[u, :, cols]

    for u in range(n_sub):
        glu_to_hbuf(u)
    y_prev = None
    for u in range(n_sub):
        for c in range(n_chunks):
            conv_chunk(u, c)
            gate_piece(u, c)
            if y_prev is not None:
                tail_piece(u - 1, y_prev, c)
        y_prev = normed(u)
    for n in range(n_chunks):
        tail_piece(n_sub - 1, y_prev, n)

    for c in range(n_chunks):
        hbuf[c, 0:CONV_HALO, :] = hbuf[c, tm:tm + CONV_HALO, :]


def _col_chunks(w):
    k, n = w.shape
    return w.reshape(k, n // MXU_COLS, MXU_COLS).transpose(1, 0, 2)


def _conv_branch(xf, p, seq_len):
    m, d = xf.shape
    tm = CONV_ROW_TILE
    n_sub = tm // CONV_SUB_ROWS
    w_spec = _const_spec((d // MXU_COLS, d, MXU_COLS))
    b_spec = _const_spec((1, d))
    return pl.pallas_call(
        functools.partial(_conv_branch_kernel, tiles_per_batch=seq_len // tm),
        grid=(m // tm,),
        in_specs=[pl.BlockSpec((tm, d), lambda t: (t, 0)),
                  w_spec, w_spec, w_spec, w_spec, b_spec, b_spec, b_spec, b_spec,
                  _const_spec(p["conv_w"].shape), b_spec, b_spec, b_spec, w_spec, w_spec],
        out_specs=pl.BlockSpec((tm, d), lambda t: (t, 0)),
        out_shape=jax.ShapeDtypeStruct((m, d), F32),
        scratch_shapes=[pltpu.VMEM((d // LANES, tm + CONV_HALO, LANES), F32),
                        pltpu.VMEM((n_sub, CONV_SUB_ROWS, d), F32),
                        pltpu.VMEM((n_sub, CONV_SUB_ROWS, d), F32),
                        pltpu.VMEM((n_sub, CONV_SUB_ROWS, d), F32),
                        pltpu.VMEM((n_sub, CONV_SUB_ROWS, d), BF16)],
        compiler_params=pltpu.CompilerParams(dimension_semantics=("arbitrary",),
                                             vmem_limit_bytes=VMEM_LIMIT),
        name="conv_branch",
    )(xf, _col_chunks(p["w_val"]), _col_chunks(p["w_glu"]), _col_chunks(p["w_az"]), _col_chunks(p["w_ga"]),
      p["b_val"], p["b_glu"], p["b_az"], p["b_ga"],
      p["conv_w"], p["conv_b"], p["cln_g"], p["cln_b"], _col_chunks(p["w_pw2"]), _col_chunks(p["w_proj_a"]))


def _attn_proj_kernel(x_ref, wk_ref, bk_ref, wqt_ref, bq_ref, wvt_ref, bv_ref,
                      cosk_ref, sink_ref, cosq_ref, sinq_ref, k_ref, km_ref, qt_ref, vt_ref):
    d = x_ref.shape[1]
    xb = x_ref[...].astype(BF16)

    k = _dot(xb, wk_ref[...]) + bk_ref[...]
    cos_k = cosk_ref[...]
    sin_k = sink_ref[...]
    for c in range(d // LANES):
        lanes = slice(c * LANES, (c + 1) * LANES)
        k_rot = k[:, lanes] * cos_k + pltpu.roll(k[:, lanes], LANES // 2, axis=1) * sin_k
        k_ref[:, lanes] = k_rot.astype(BF16)
        km_ref[:, lanes] = jnp.broadcast_to(jnp.mean(k_rot, axis=0, keepdims=True), (SUBLANES, LANES))

    qt = _dot_nt(wqt_ref[...], xb) + bq_ref[...]
    cos_q = cosq_ref[...]
    sin_q = sinq_ref[...]
    scale = LOG2_E / math.sqrt(HEAD_DIM)
    for c in range(d // LANES):
        rows = slice(c * LANES, (c + 1) * LANES)
        q = qt[rows]
        partner = jnp.concatenate([q[LANES // 2:], q[:LANES // 2]], axis=0)
        q_rot = (q * cos_q + partner * sin_q) * scale
        qt_ref[rows, :] = q_rot.astype(BF16)

    vt = _dot_nt(wvt_ref[...], xb) + bv_ref[...]
    vt_ref[...] = vt.astype(BF16)


def _attn_proj(xf, p, tables, batch, seq_len):
    m, d = xf.shape
    tm = MOBA_BLOCK
    nb = seq_len // tm
    w_spec = _const_spec((d, d))
    row_spec = _const_spec((1, d))
    col_spec = _const_spec((d, 1))
    cos_k, sin_k, cos_q, sin_q = tables
    k, km, qt, vt = pl.pallas_call(
        _attn_proj_kernel,
        grid=(m // tm,),
        in_specs=[pl.BlockSpec((tm, d), lambda t: (t, 0)),
                  w_spec, row_spec, w_spec, col_spec, w_spec, col_spec,
                  pl.BlockSpec((tm, LANES), lambda t: (t % nb, 0)),
                  pl.BlockSpec((tm, LANES), lambda t: (t % nb, 0)),
                  pl.BlockSpec((LANES, tm), lambda t: (0, t % nb)),
                  pl.BlockSpec((LANES, tm), lambda t: (0, t % nb))],
        out_specs=[pl.BlockSpec((None, None, tm, d), lambda t: (t // nb, t % nb, 0, 0)),
                   pl.BlockSpec((None, None, SUBLANES, d), lambda t: (t // nb, t % nb, 0, 0)),
                   pl.BlockSpec((None, None, d, tm), lambda t: (t // nb, t % nb, 0, 0)),
                   pl.BlockSpec((None, None, d, tm), lambda t: (t // nb, t % nb, 0, 0))],
        out_shape=[jax.ShapeDtypeStruct((batch, nb, tm, d), BF16),
                   jax.ShapeDtypeStruct((batch, nb, SUBLANES, d), F32),
                   jax.ShapeDtypeStruct((batch, nb, d, tm), BF16),
                   jax.ShapeDtypeStruct((batch, nb, d, tm), BF16)],
        compiler_params=pltpu.CompilerParams(dimension_semantics=("arbitrary",),
                                             vmem_limit_bytes=VMEM_LIMIT),
        name="attn_proj",
    )(xf, p["w_k"], p["b_k"], p["w_q_t"], p["b_q"], p["w_v_t"], p["b_v"],
      cos_k, sin_k, cos_q, sin_q)
    return k, km[:, :, 0, :], qt, vt


def _select_bias(gate, own, nb):
    blk = lax.broadcasted_iota(jnp.int32, gate.shape, 0)
    past = blk < own
    g = jnp.where(past, gate, NEG_INF)
    sel = jnp.zeros(gate.shape, jnp.bool_)
    for _ in range(min(MOBA_TOP_K, nb)):
        best = jnp.max(g, axis=0, keepdims=True)
        first = jnp.min(jnp.where(g == best, blk, nb), axis=0, keepdims=True)
        pick = blk == first
        sel = jnp.logical_or(sel, pick)
        g = jnp.where(pick, -jnp.inf, g)
    return jnp.where(jnp.logical_and(sel, past), 0.0, NEG_INF)


def _moba_kernel(qt_ref, k_ref, vt_ref, km_ref, ot_ref, qh_scr, bias_scr, m_scr, acc_scr, s_even, s_odd,
                 smax_even, smax_odd):
    nb = k_ref.shape[0]
    tk = k_ref.shape[1]
    tq = qt_ref.shape[1]
    n_heads = qt_ref.shape[0] // HEAD_DIM
    own = pl.program_id(2)

    feat_row = lax.broadcasted_iota(jnp.int32, (LANES, tq), 0)
    feat_lane = lax.broadcasted_iota(jnp.int32, (nb, LANES), 1)
    key_pos = lax.broadcasted_iota(jnp.int32, (tk, tq), 0)
    qry_pos = lax.broadcasted_iota(jnp.int32, (tk, tq), 1)
    ones = jnp.ones((ONES_ROWS, tk), BF16)

    def pair_of(h):
        return slice((h // HEADS_PER_STEP) * LANES, (h // HEADS_PER_STEP + 1) * LANES)

    def v_aug(j, h):
        return jnp.concatenate([vt_ref[j, h * HEAD_DIM:(h + 1) * HEAD_DIM, :], ones], axis=0)

    def past_scores(j, h):
        return _dot(k_ref[j, :, pair_of(h)], qh_scr[h])

    def stash(s_ref, smax_ref, h, s):
        s_ref[h] = s
        smax_ref[h] = jnp.broadcast_to(jnp.max(s, axis=0, keepdims=True), (SUBLANES, tq))

    def absorb(j, h, s_ref, smax_ref, bias):
        m = m_scr[h]
        m_new = jnp.maximum(m, smax_ref[h] + bias)
        p = jnp.exp2((s_ref[h] - (m_new - bias)[0:1]).astype(BF16))
        acc_scr[h] = acc_scr[h] * jnp.exp2(m - m_new)[0:1] + _dot(v_aug(j, h), p)
        m_scr[h] = m_new

    for h in range(n_heads):
        q2 = qt_ref[pair_of(h), :]
        km = km_ref[:, pair_of(h)]
        qh = jnp.where((feat_row // HALF) % HEADS_PER_STEP == h % HEADS_PER_STEP, q2, jnp.zeros_like(q2))
        kmh = jnp.where((feat_lane // HALF) % HEADS_PER_STEP == h % HEADS_PER_STEP, km, 0.0).astype(BF16)
        gate = _dot(kmh, q2)
        bias = _select_bias(gate, own, nb)
        for jb in range(nb):
            bias_scr[h, jb] = jnp.broadcast_to(bias[jb:jb + 1], (SUBLANES, tq))
        qh_scr[h] = qh
        s_own = _dot(k_ref[own, :, pair_of(h)], qh)
        stash(s_odd, smax_odd, h, jnp.where(key_pos <= qry_pos, s_own, NEG_INF))
        m_scr[h] = jnp.full((SUBLANES, tq), NEG_INF, F32)
        acc_scr[h] = jnp.zeros(acc_scr.shape[1:], F32)

    for h in range(n_heads):
        stash(s_even, smax_even, h, past_scores(0, h))
        absorb(own, h, s_odd, smax_odd, jnp.zeros((SUBLANES, tq), F32))

    def two_blocks(t, carry):
        j0 = 2 * t
        j1 = jnp.minimum(j0 + 1, nb - 1)
        j2 = jnp.minimum(j0 + 2, nb - 1)
        for h in range(n_heads):
            stash(s_odd, smax_odd, h, past_scores(j1, h))
            absorb(j0, h, s_even, smax_even, bias_scr[h, j0])
        for h in range(n_heads):
            stash(s_even, smax_even, h, past_scores(j2, h))
            absorb(j1, h, s_odd, smax_odd, bias_scr[h, j1])
        return carry

    lax.fori_loop(0, (own + 1) // 2, two_blocks, 0)
    for h in range(n_heads):
        ot_ref[h * HEAD_DIM:(h + 1) * HEAD_DIM, :] = (acc_scr[h, 0:HEAD_DIM, :]
                                                       / acc_scr[h, HEAD_DIM:HEAD_DIM + 1, :])


def _moba_attention(k, km, qt, vt):
    batch, nb, tk, d = k.shape
    tq = qt.shape[3]
    width = ATTN_HEADS_PER_STEP * HEAD_DIM
    resident = pl.Buffered(1)
    return pl.pallas_call(
        _moba_kernel,
        grid=(batch, d // width, nb),
        in_specs=[pl.BlockSpec((None, None, width, tq), lambda b, g, i: (b, i, g, 0)),
                  pl.BlockSpec((None, nb, tk, width), lambda b, g, i: (b, 0, 0, g), pipeline_mode=resident),
                  pl.BlockSpec((None, nb, width, tk), lambda b, g, i: (b, 0, g, 0), pipeline_mode=resident),
                  pl.BlockSpec((None, nb, width), lambda b, g, i: (b, 0, g))],
        out_specs=pl.BlockSpec((None, None, width, tq), lambda b, g, i: (b, i, g, 0)),
        out_shape=jax.ShapeDtypeStruct((batch, nb, d, tq), F32),
        scratch_shapes=[pltpu.VMEM((ATTN_HEADS_PER_STEP, LANES, tq), BF16),
                        pltpu.VMEM((ATTN_HEADS_PER_STEP, nb, SUBLANES, tq), F32),
                        pltpu.VMEM((ATTN_HEADS_PER_STEP, SUBLANES, tq), F32),
                        pltpu.VMEM((ATTN_HEADS_PER_STEP, HEAD_DIM + ONES_ROWS, tq), F32),
                        pltpu.VMEM((ATTN_HEADS_PER_STEP, tk, tq), F32),
                        pltpu.VMEM((ATTN_HEADS_PER_STEP, tk, tq), F32),
                        pltpu.VMEM((ATTN_HEADS_PER_STEP, SUBLANES, tq), F32),
                        pltpu.VMEM((ATTN_HEADS_PER_STEP, SUBLANES, tq), F32)],
        compiler_params=pltpu.CompilerParams(dimension_semantics=("arbitrary", "arbitrary", "arbitrary"),
                                             vmem_limit_bytes=VMEM_LIMIT),
        name="moba_attn",
    )(qt, k, vt, km)


def _merge_kernel(x_ref, ot_ref, ya_ref, wbz_ref, wgb_ref, bbz_ref, bgb_ref, wpb_ref, wout_ref,
                  g_ref, beta_ref, out_ref, *, alpha):
    x = x_ref[...]
    xb = x.astype(BF16)
    o = ot_ref[...].T
    b_z = _dot(xb, wbz_ref[...]) + bbz_ref[...]
    y_b = _dot((o * _silu(b_z)).astype(BF16), wpb_ref[...])
    g_b = _dot(xb, wgb_ref[...]) + bgb_ref[...]
    merged = ya_ref[...] + _sigmoid(g_b) * y_b
    out = _dot(merged.astype(BF16), wout_ref[...])
    out_ref[...] = _layer_norm(alpha * x + out, g_ref[...], beta_ref[...])


def _merge(xf, ot, ya, p, alpha):
    m, d = xf.shape
    batch, nb, _, tm = ot.shape
    w_spec = _const_spec((d, d))
    b_spec = _const_spec((1, d))
    return pl.pallas_call(
        functools.partial(_merge_kernel, alpha=alpha),
        grid=(m // tm,),
        in_specs=[pl.BlockSpec((tm, d), lambda t: (t, 0)),
                  pl.BlockSpec((None, None, d, tm), lambda t: (t // nb, t % nb, 0, 0)),
                  pl.BlockSpec((tm, d), lambda t: (t, 0)),
                  w_spec, w_spec, b_spec, b_spec, w_spec, w_spec, b_spec, b_spec],
        out_specs=pl.BlockSpec((tm, d), lambda t: (t, 0)),
        out_shape=jax.ShapeDtypeStruct((m, d), F32),
        compiler_params=pltpu.CompilerParams(dimension_semantics=("arbitrary",),
                                             vmem_limit_bytes=VMEM_LIMIT),
        name="merge",
    )(xf, ot, ya, p["w_bz"], p["w_gb"], p["b_bz"], p["b_gb"], p["w_proj_b"], p["w_out"],
      p["ln_g"], p["ln_b"])


def _rope_tables(seq_len):
    inv_freq = np.float32(ROPE_THETA) ** (-np.arange(HALF, dtype=np.float32) / np.float32(HALF))
    ang = np.arange(seq_len, dtype=np.float32)[:, None] * inv_freq[None, :].astype(np.float32)
    cos, sin = jnp.asarray(np.cos(ang), F32), jnp.asarray(np.sin(ang), F32)
    cos_k = jnp.tile(cos, (1, 2 * HEADS_PER_STEP))
    sin_k = jnp.concatenate([-sin] * HEADS_PER_STEP + [sin] * HEADS_PER_STEP, axis=1)
    return cos_k, sin_k, cos_k.T, sin_k.T


def _layer_params(l, w_in, b_in, conv_w, conv_b, conv_ln_g, conv_ln_b, w_pw2, w_proj_a, w_proj_b, w_out,
                  ln_g, ln_b):
    d = w_in.shape[1]
    w = w_in[l].reshape(d, -1, d)
    b = b_in[l].reshape(-1, d)
    names = ("val", "glu", "az", "q", "k", "v", "bz", "ga", "gb")
    wg = {n: w[:, i, :] for i, n in enumerate(names)}
    bg = {n: b[i] for i, n in enumerate(names)}
    pair_layout = lambda v: (v.reshape(v.shape[:-1] + (N_HEADS // HEADS_PER_STEP, HEADS_PER_STEP, 2, HALF))
                             .swapaxes(-3, -2).reshape(v.shape))
    row = lambda v: v.reshape(1, d).astype(F32)
    return {
        "w_val": wg["val"].astype(BF16), "w_glu": wg["glu"].astype(BF16), "w_az": wg["az"].astype(BF16),
        "w_ga": wg["ga"].astype(BF16), "w_bz": wg["bz"].astype(BF16), "w_gb": wg["gb"].astype(BF16),
        "b_val": row(bg["val"]), "b_glu": row(bg["glu"]), "b_az": row(bg["az"]),
        "b_ga": row(bg["ga"]), "b_bz": row(bg["bz"]), "b_gb": row(bg["gb"]),
        "w_k": pair_layout(wg["k"]).astype(BF16), "b_k": row(pair_layout(bg["k"])),
        "w_q_t": pair_layout(wg["q"]).T.astype(BF16), "b_q": pair_layout(bg["q"]).reshape(d, 1),
        "w_v_t": wg["v"].T.astype(BF16), "b_v": bg["v"].reshape(d, 1),
        "conv_w": jnp.pad(conv_w[l], ((0, 1), (0, 0))), "conv_b": row(conv_b[l]),
        "cln_g": row(conv_ln_g[l]), "cln_b": row(conv_ln_b[l]),
        "w_pw2": w_pw2[l].astype(BF16), "w_proj_a": w_proj_a[l].astype(BF16),
        "w_proj_b": w_proj_b[l].astype(BF16), "w_out": w_out[l].astype(BF16),
        "ln_g": row(ln_g[l]), "ln_b": row(ln_b[l]),
    }


def kernel(x, w_in, b_in, conv_w, conv_b, conv_ln_g, conv_ln_b, w_pw2, w_proj_a, w_proj_b, w_out, ln_g, ln_b):
    batch, seq_len, d = x.shape
    depth = w_in.shape[0]
    assert d == N_HEADS * HEAD_DIM and w_in.shape[2] == 9 * d
    assert seq_len % MOBA_BLOCK == 0 and seq_len // MOBA_BLOCK <= LANES
    assert seq_len % CONV_ROW_TILE == 0 and conv_w.shape[1] == CONV_KERNEL
    alpha = (2 * depth) ** 0.25
    tables = _rope_tables(seq_len)
    xf = x.reshape(batch * seq_len, d)
    for l in range(depth):
        p = _layer_params(l, w_in, b_in, conv_w, conv_b, conv_ln_g, conv_ln_b, w_pw2, w_proj_a, w_proj_b,
                          w_out, ln_g, ln_b)
        ya = _conv_branch(xf, p, seq_len)
        k, km, qt, vt = _attn_proj(xf, p, tables, batch, seq_len)
        ot = _moba_attention(k, km, qt, vt)
        xf = _merge(xf, ot, ya, p, alpha)
    return xf.reshape(batch, seq_len, d)
```

```python
import functools
import math

import jax
import jax.numpy as jnp
import numpy as np
from jax import lax
from jax.experimental import pallas as pl
from jax.experimental.pallas import tpu as pltpu

N_HEADS = 16
HEAD_DIM = 64
HALF = HEAD_DIM // 2
MOBA_BLOCK = 256
MOBA_TOP_K = 3
CONV_KERNEL = 31
ROPE_THETA = 10000.0
LN_EPS = 1e-5
NEG_INF = -1e30
LOG2_E = math.log2(math.e)
G_VAL, G_GLU, G_AZ, G_Q, G_K, G_V, G_BZ, G_GA, G_GB = range(9)
N_GROUPS = 9

LANES = 128
SUBLANES = 8
HEADS_PER_STEP = LANES // HEAD_DIM
CONV_HALO = 32
CONV_ROWS = 64
CONV_SUB_ROWS = 256
CONV_ROW_TILE = 512
MXU_COLS = 256
ATTN_HEADS_PER_STEP = 16
ONES_ROWS = 16
ROW_TILE = MOBA_BLOCK
VMEM_LIMIT = 56 * 1024 * 1024

BF16 = jnp.bfloat16
F32 = jnp.float32


def _dot(a, b):
    return jnp.dot(a, b, preferred_element_type=F32)


def _dot_nt(a, b):
    return lax.dot_general(a, b, (((1,), (1,)), ((), ())), preferred_element_type=F32)


def _sigmoid(v):
    return 1.0 / (1.0 + jnp.exp(-v))


def _silu(v):
    return v * _sigmoid(v)


def _layer_norm(v, g, b):
    mu = jnp.mean(v, axis=-1, keepdims=True)
    vc = v - mu
    var = jnp.mean(vc * vc, axis=-1, keepdims=True)
    return vc * lax.rsqrt(var + LN_EPS) * g + b


def _conv_branch_kernel(x_ref, wval_ref, wglu_ref, wz_ref, wga_ref, bin_ref,
                        cw_ref, cb_ref, cg_ref, cbeta_ref, wpw2_ref, wpa_ref, out_ref,
                        hbuf, cbuf, za_buf, ga_buf, y2_buf, *, tiles_per_batch):
    tm, d = x_ref.shape
    n_sub = tm // CONV_SUB_ROWS
    n_chunks = d // LANES
    n_cols = d // MXU_COLS
    assert n_chunks == 2 * n_cols
    t = pl.program_id(0)
    bval_ref, bglu_ref, bz_ref, bga_ref = (bin_ref.at[g:g + 1] for g in (G_VAL, G_GLU, G_AZ, G_GA))

    @pl.when(t % tiles_per_batch == 0)
    def _():
        hbuf[:, 0:CONV_HALO, :] = jnp.zeros((n_chunks, CONV_HALO, LANES), F32)

    xb = [x_ref[u * CONV_SUB_ROWS:(u + 1) * CONV_SUB_ROWS, :].astype(BF16) for u in range(n_sub)]

    def glu_to_hbuf(u):
        for n in range(n_cols):
            cols = slice(n * MXU_COLS, (n + 1) * MXU_COLS)
            a_val = _dot(xb[u], wval_ref[n]) + bval_ref[:, cols]
            a_glu = _dot(xb[u], wglu_ref[n]) + bglu_ref[:, cols]
            h = a_val * _sigmoid(a_glu)
            for c2 in range(MXU_COLS // LANES):
                rows = slice(CONV_HALO + u * CONV_SUB_ROWS, CONV_HALO + (u + 1) * CONV_SUB_ROWS)
                hbuf[n * (MXU_COLS // LANES) + c2, rows, :] = h[:, c2 * LANES:(c2 + 1) * LANES]

    def conv_chunk(u, c):
        first = CONV_HALO - (CONV_KERNEL - 1)
        lanes = slice(c * LANES, (c + 1) * LANES)
        for r in range(CONV_SUB_ROWS // CONV_ROWS):
            base = u * CONV_SUB_ROWS + r * CONV_ROWS + first
            acc = jnp.broadcast_to(cb_ref[:, lanes], (CONV_ROWS, LANES))
            for k in range(CONV_KERNEL):
                acc = acc + hbuf[c, base + k:base + k + CONV_ROWS, :] * cw_ref[k:k + 1, lanes]
            cbuf[u, r * CONV_ROWS:(r + 1) * CONV_ROWS, lanes] = acc

    def gate_piece(u, n):
        if n < n_cols:
            cols = slice(n * MXU_COLS, (n + 1) * MXU_COLS)
            za_buf[u, :, cols] = _silu(_dot(xb[u], wz_ref[n % n_cols]) + bz_ref[:, cols])
        else:
            cols = slice((n - n_cols) * MXU_COLS, (n - n_cols + 1) * MXU_COLS)
            ga_buf[u, :, cols] = _sigmoid(_dot(xb[u], wga_ref[n % n_cols]) + bga_ref[:, cols])

    def normed(u):
        return _silu(_layer_norm(cbuf[u], cg_ref[...], cbeta_ref[...])).astype(BF16)

    def tail_piece(u, y, n):
        if n < n_cols:
            cols = slice(n * MXU_COLS, (n + 1) * MXU_COLS)
            y2_buf[u, :, cols] = (_dot(y, wpw2_ref[n % n_cols]) * za_buf[u, :, cols]).astype(BF16)
        else:
            cols = slice((n - n_cols) * MXU_COLS, (n - n_cols + 1) * MXU_COLS)
            rows = slice(u * CONV_SUB_ROWS, (u + 1) * CONV_SUB_ROWS)
            out_ref[rows, cols] = _dot(y2_buf[u], wpa_ref[n % n_cols]) * ga_buf[u, :, cols]

    for u in range(n_sub):
        glu_to_hbuf(u)
    y_prev = None
    for u in range(n_sub):
        for c in range(n_chunks):
            conv_chunk(u, c)
            gate_piece(u, c)
            if y_prev is not None:
                tail_piece(u - 1, y_prev, c)
        y_prev = normed(u)
    for n in range(n_chunks):
        tail_piece(n_sub - 1, y_prev, n)

    for c in range(n_chunks):
        hbuf[c, 0:CONV_HALO, :] = hbuf[c, tm:tm + CONV_HALO, :]


def _col_chunks(w):
    k, n = w.shape[-2:]
    return jnp.swapaxes(w.reshape(w.shape[:-1] + (n // MXU_COLS, MXU_COLS)), -3, -2)


def _group_spec(l, g, d):
    n = d // MXU_COLS
    return pl.BlockSpec((None, n, d, MXU_COLS), lambda *_: (l, g, 0, 0), pipeline_mode=pl.Buffered(1))


def _layer_spec(l, shape):
    return pl.BlockSpec((None,) + shape, lambda *_: (l,) + (0,) * len(shape), pipeline_mode=pl.Buffered(1))


def _conv_branch(xf, w, l, seq_len):
    m, d = xf.shape
    tm = CONV_ROW_TILE
    n_sub = tm // CONV_SUB_ROWS
    chunked = (d // MXU_COLS, d, MXU_COLS)
    row = (1, d)
    return pl.pallas_call(
        functools.partial(_conv_branch_kernel, tiles_per_batch=seq_len // tm),
        grid=(m // tm,),
        in_specs=[pl.BlockSpec((tm, d), lambda t: (t, 0)),
                  _group_spec(l, G_VAL, d), _group_spec(l, G_GLU, d), _group_spec(l, G_AZ, d),
                  _group_spec(l, G_GA, d), _layer_spec(l, (N_GROUPS, d)),
                  _layer_spec(l, w["conv_w"].shape[1:]), _layer_spec(l, row), _layer_spec(l, row),
                  _layer_spec(l, row), _layer_spec(l, chunked), _layer_spec(l, chunked)],
        out_specs=pl.BlockSpec((tm, d), lambda t: (t, 0)),
        out_shape=jax.ShapeDtypeStruct((m, d), F32),
        scratch_shapes=[pltpu.VMEM((d // LANES, tm + CONV_HALO, LANES), F32),
                        pltpu.VMEM((n_sub, CONV_SUB_ROWS, d), F32),
                        pltpu.VMEM((n_sub, CONV_SUB_ROWS, d), F32),
                        pltpu.VMEM((n_sub, CONV_SUB_ROWS, d), F32),
                        pltpu.VMEM((n_sub, CONV_SUB_ROWS, d), BF16)],
        compiler_params=pltpu.CompilerParams(dimension_semantics=("arbitrary",),
                                             vmem_limit_bytes=VMEM_LIMIT),
        name="conv_branch",
    )(xf, w["w_in"], w["w_in"], w["w_in"], w["w_in"], w["b_in"],
      w["conv_w"], w["conv_b"], w["cln_g"], w["cln_b"], w["w_pw2"], w["w_proj_a"])


def _attn_proj_kernel(x_ref, wk_ref, bk_ref, wqt_ref, bq_ref, wvt_ref, bv_ref,
                      cosk_ref, sink_ref, cosq_ref, sinq_ref, k_ref, km_ref, qt_ref, vt_ref):
    d = x_ref.shape[1]
    xb = x_ref[...].astype(BF16)

    k = jnp.concatenate([_dot(xb, wk_ref[n]) for n in range(wk_ref.shape[0])], axis=1) + bk_ref[...]
    cos_k = cosk_ref[...]
    sin_k = sink_ref[...]
    for c in range(d // LANES):
        lanes = slice(c * LANES, (c + 1) * LANES)
        k_rot = k[:, lanes] * cos_k + pltpu.roll(k[:, lanes], LANES // 2, axis=1) * sin_k
        k_ref[:, lanes] = k_rot.astype(BF16)
        km_ref[:, lanes] = jnp.broadcast_to(jnp.mean(k_rot, axis=0, keepdims=True), (SUBLANES, LANES))

    qt = _dot_nt(wqt_ref[...], xb) + bq_ref[...]
    cos_q = cosq_ref[...]
    sin_q = sinq_ref[...]
    scale = LOG2_E / math.sqrt(HEAD_DIM)
    for c in range(d // LANES):
        rows = slice(c * LANES, (c + 1) * LANES)
        q = qt[rows]
        partner = jnp.concatenate([q[LANES // 2:], q[:LANES // 2]], axis=0)
        q_rot = (q * cos_q + partner * sin_q) * scale
        qt_ref[rows, :] = q_rot.astype(BF16)

    vt = _dot_nt(wvt_ref[...], xb) + bv_ref[...]
    vt_ref[...] = vt.astype(BF16)


def _attn_proj(xf, w, l, tables, batch, seq_len):
    m, d = xf.shape
    tm = MOBA_BLOCK
    nb = seq_len // tm
    w_spec = _layer_spec(l, (d, d))
    row_spec = _layer_spec(l, (1, d))
    col_spec = _layer_spec(l, (d, 1))
    cos_k, sin_k, cos_q, sin_q = tables
    k, km, qt, vt = pl.pallas_call(
        _attn_proj_kernel,
        grid=(m // tm,),
        in_specs=[pl.BlockSpec((tm, d), lambda t: (t, 0)),
                  _layer_spec(l, (d // MXU_COLS, d, MXU_COLS)), row_spec, w_spec, col_spec, w_spec, col_spec,
                  pl.BlockSpec((tm, LANES), lambda t: (t % nb, 0)),
                  pl.BlockSpec((tm, LANES), lambda t: (t % nb, 0)),
                  pl.BlockSpec((LANES, tm), lambda t: (0, t % nb)),
                  pl.BlockSpec((LANES, tm), lambda t: (0, t % nb))],
        out_specs=[pl.BlockSpec((None, None, tm, d), lambda t: (t // nb, t % nb, 0, 0)),
                   pl.BlockSpec((None, None, SUBLANES, d), lambda t: (t // nb, t % nb, 0, 0)),
                   pl.BlockSpec((None, None, d, tm), lambda t: (t // nb, t % nb, 0, 0)),
                   pl.BlockSpec((None, None, d, tm), lambda t: (t // nb, t % nb, 0, 0))],
        out_shape=[jax.ShapeDtypeStruct((batch, nb, tm, d), BF16),
                   jax.ShapeDtypeStruct((batch, nb, SUBLANES, d), F32),
                   jax.ShapeDtypeStruct((batch, nb, d, tm), BF16),
                   jax.ShapeDtypeStruct((batch, nb, d, tm), BF16)],
        compiler_params=pltpu.CompilerParams(dimension_semantics=("arbitrary",),
                                             vmem_limit_bytes=VMEM_LIMIT),
        name="attn_proj",
    )(xf, w["w_k"], w["b_k"], w["w_q_t"], w["b_q"], w["w_v_t"], w["b_v"],
      cos_k, sin_k, cos_q, sin_q)
    return k, km[:, :, 0, :], qt, vt


def _select_bias(gate, own, nb):
    blk = lax.broadcasted_iota(jnp.int32, gate.shape, 0)
    past = blk < own
    g = jnp.where(past, gate, NEG_INF)
    sel = jnp.zeros(gate.shape, jnp.bool_)
    for _ in range(min(MOBA_TOP_K, nb)):
        best = jnp.max(g, axis=0, keepdims=True)
        first = jnp.min(jnp.where(g == best, blk, nb), axis=0, keepdims=True)
        pick = blk == first
        sel = jnp.logical_or(sel, pick)
        g = jnp.where(pick, -jnp.inf, g)
    return jnp.where(jnp.logical_and(sel, past), 0.0, NEG_INF)


def _moba_kernel(qt_ref, k_ref, vt_ref, km_ref, ot_ref, qh_scr, bias_scr, m_scr, acc_scr, s_even, s_odd,
                 smax_even, smax_odd):
    nb = k_ref.shape[0]
    tk = k_ref.shape[1]
    tq = qt_ref.shape[1]
    n_heads = qt_ref.shape[0] // HEAD_DIM
    own = pl.program_id(2)

    feat_row = lax.broadcasted_iota(jnp.int32, (LANES, tq), 0)
    feat_lane = lax.broadcasted_iota(jnp.int32, (nb, LANES), 1)
    key_pos = lax.broadcasted_iota(jnp.int32, (tk, tq), 0)
    qry_pos = lax.broadcasted_iota(jnp.int32, (tk, tq), 1)
    ones = jnp.ones((ONES_ROWS, tk), BF16)

    def pair_of(h):
        return slice((h // HEADS_PER_STEP) * LANES, (h // HEADS_PER_STEP + 1) * LANES)

    def v_aug(j, h):
        return jnp.concatenate([vt_ref[j, h * HEAD_DIM:(h + 1) * HEAD_DIM, :], ones], axis=0)

    def past_scores(j, h):
        return _dot(k_ref[j, :, pair_of(h)], qh_scr[h])

    def stash(s_ref, smax_ref, h, s):
        s_ref[h] = s
        smax_ref[h] = jnp.broadcast_to(jnp.max(s, axis=0, keepdims=True), (SUBLANES, tq))

    def absorb(j, h, s_ref, smax_ref, bias):
        m = m_scr[h]
        m_new = jnp.maximum(m, smax_ref[h] + bias)
        p = jnp.exp2((s_ref[h] - (m_new - bias)[0:1]).astype(BF16))
        acc_scr[h] = acc_scr[h] * jnp.exp2(m - m_new)[0:1] + _dot(v_aug(j, h), p)
        m_scr[h] = m_new

    for h in range(n_heads):
        q2 = qt_ref[pair_of(h), :]
        km = km_ref[:, pair_of(h)]
        qh = jnp.where((feat_row // HALF) % HEADS_PER_STEP == h % HEADS_PER_STEP, q2, jnp.zeros_like(q2))
        kmh = jnp.where((feat_lane // HALF) % HEADS_PER_STEP == h % HEADS_PER_STEP, km, 0.0).astype(BF16)
        gate = _dot(kmh, q2)
        bias = _select_bias(gate, own, nb)
        for jb in range(nb):
            bias_scr[h, jb] = jnp.broadcast_to(bias[jb:jb + 1], (SUBLANES, tq))
        qh_scr[h] = qh
        s_own = _dot(k_ref[own, :, pair_of(h)], qh)
        stash(s_odd, smax_odd, h, jnp.where(key_pos <= qry_pos, s_own, NEG_INF))
        m_scr[h] = jnp.full((SUBLANES, tq), NEG_INF, F32)
        acc_scr[h] = jnp.zeros(acc_scr.shape[1:], F32)

    for h in range(n_heads):
        stash(s_even, smax_even, h, past_scores(0, h))
        absorb(own, h, s_odd, smax_odd, jnp.zeros((SUBLANES, tq), F32))

    def two_blocks(t, carry):
        j0 = 2 * t
        j1 = jnp.minimum(j0 + 1, nb - 1)
        j2 = jnp.minimum(j0 + 2, nb - 1)
        for h in range(n_heads):
            stash(s_odd, smax_odd, h, past_scores(j1, h))
            absorb(j0, h, s_even, smax_even, bias_scr[h, j0])
        for h in range(n_heads):
            stash(s_even, smax_even, h, past_scores(j2, h))
            absorb(j1, h, s_odd, smax_odd, bias_scr[h, j1])
        return carry

    lax.fori_loop(0, (own + 1) // 2, two_blocks, 0)
    for h in range(n_heads):
        ot_ref[h * HEAD_DIM:(h + 1) * HEAD_DIM, :] = (acc_scr[h, 0:HEAD_DIM, :]
                                                       / acc_scr[h, HEAD_DIM:HEAD_DIM + 1, :])


def _moba_attention(k, km, qt, vt):
    batch, nb, tk, d = k.shape
    tq = qt.shape[3]
    width = ATTN_HEADS_PER_STEP * HEAD_DIM
    resident = pl.Buffered(1)
    return pl.pallas_call(
        _moba_kernel,
        grid=(batch, d // width, nb),
        in_specs=[pl.BlockSpec((None, None, width, tq), lambda b, g, i: (b, i, g, 0)),
                  pl.BlockSpec((None, nb, tk, width), lambda b, g, i: (b, 0, 0, g), pipeline_mode=resident),
                  pl.BlockSpec((None, nb, width, tk), lambda b, g, i: (b, 0, g, 0), pipeline_mode=resident),
                  pl.BlockSpec((None, nb, width), lambda b, g, i: (b, 0, g))],
        out_specs=pl.BlockSpec((None, None, width, tq), lambda b, g, i: (b, i, g, 0)),
        out_shape=jax.ShapeDtypeStruct((batch, nb, d, tq), F32),
        scratch_shapes=[pltpu.VMEM((ATTN_HEADS_PER_STEP, LANES, tq), BF16),
                        pltpu.VMEM((ATTN_HEADS_PER_STEP, nb, SUBLANES, tq), F32),
                        pltpu.VMEM((ATTN_HEADS_PER_STEP, SUBLANES, tq), F32),
                        pltpu.VMEM((ATTN_HEADS_PER_STEP, HEAD_DIM + ONES_ROWS, tq), F32),
                        pltpu.VMEM((ATTN_HEADS_PER_STEP, tk, tq), F32),
                        pltpu.VMEM((ATTN_HEADS_PER_STEP, tk, tq), F32),
                        pltpu.VMEM((ATTN_HEADS_PER_STEP, SUBLANES, tq), F32),
                        pltpu.VMEM((ATTN_HEADS_PER_STEP, SUBLANES, tq), F32)],
        compiler_params=pltpu.CompilerParams(dimension_semantics=("arbitrary", "arbitrary", "arbitrary"),
                                             vmem_limit_bytes=VMEM_LIMIT),
        name="moba_attn",
    )(qt, k, vt, km)


def _merge_kernel(x_ref, ot_ref, ya_ref, wbz_ref, wgb_ref, bin_ref, wpb_ref, wout_ref,
                  g_ref, beta_ref, out_ref, *, alpha):
    x = x_ref[...]
    xb = x.astype(BF16)
    o = ot_ref[...].T
    n_cols = wbz_ref.shape[0]
    b_z = jnp.concatenate([_dot(xb, wbz_ref[n]) for n in range(n_cols)], axis=1) + bin_ref[G_BZ:G_BZ + 1, :]
    y_b = _dot((o * _silu(b_z)).astype(BF16), wpb_ref[...])
    g_b = jnp.concatenate([_dot(xb, wgb_ref[n]) for n in range(n_cols)], axis=1) + bin_ref[G_GB:G_GB + 1, :]
    merged = ya_ref[...] + _sigmoid(g_b) * y_b
    out = _dot(merged.astype(BF16), wout_ref[...])
    out_ref[...] = _layer_norm(alpha * x + out, g_ref[...], beta_ref[...])


def _merge(xf, ot, ya, w, l, alpha):
    m, d = xf.shape
    batch, nb, _, tm = ot.shape
    w_spec = _layer_spec(l, (d, d))
    b_spec = _layer_spec(l, (1, d))
    return pl.pallas_call(
        functools.partial(_merge_kernel, alpha=alpha),
        grid=(m // tm,),
        in_specs=[pl.BlockSpec((tm, d), lambda t: (t, 0)),
                  pl.BlockSpec((None, None, d, tm), lambda t: (t // nb, t % nb, 0, 0)),
                  pl.BlockSpec((tm, d), lambda t: (t, 0)),
                  _group_spec(l, G_BZ, d), _group_spec(l, G_GB, d), _layer_spec(l, (N_GROUPS, d)),
                  w_spec, w_spec, b_spec, b_spec],
        out_specs=pl.BlockSpec((tm, d), lambda t: (t, 0)),
        out_shape=jax.ShapeDtypeStruct((m, d), F32),
        compiler_params=pltpu.CompilerParams(dimension_semantics=("arbitrary",),
                                             vmem_limit_bytes=VMEM_LIMIT),
        name="merge",
    )(xf, ot, ya, w["w_in"], w["w_in"], w["b_in"], w["w_proj_b"], w["w_out"], w["ln_g"], w["ln_b"])


def _rope_tables(seq_len):
    inv_freq = np.float32(ROPE_THETA) ** (-np.arange(HALF, dtype=np.float32) / np.float32(HALF))
    ang = np.arange(seq_len, dtype=np.float32)[:, None] * inv_freq[None, :].astype(np.float32)
    cos, sin = jnp.asarray(np.cos(ang), F32), jnp.asarray(np.sin(ang), F32)
    cos_k = jnp.tile(cos, (1, 2 * HEADS_PER_STEP))
    sin_k = jnp.concatenate([-sin] * HEADS_PER_STEP + [sin] * HEADS_PER_STEP, axis=1)
    return cos_k, sin_k, cos_k.T, sin_k.T


def _prepare_params(w_in, b_in, conv_w, conv_b, conv_ln_g, conv_ln_b, w_pw2, w_proj_a, w_proj_b, w_out,
                    ln_g, ln_b):
    depth, d, _ = w_in.shape
    group = lambda g: w_in[:, :, g * d:(g + 1) * d]
    bias = b_in.reshape(depth, N_GROUPS, d)
    pair_layout = lambda v: (v.reshape(v.shape[:-1] + (N_HEADS // HEADS_PER_STEP, HEADS_PER_STEP, 2, HALF))
                             .swapaxes(-3, -2).reshape(v.shape))
    row = lambda v: v.reshape(depth, 1, d)
    return {
        "w_in": _col_chunks(w_in).astype(BF16),
        "b_in": bias,
        "w_k": _col_chunks(pair_layout(group(G_K))).astype(BF16), "b_k": row(pair_layout(bias[:, G_K])),
        "w_q_t": pair_layout(group(G_Q)).swapaxes(1, 2).astype(BF16),
        "b_q": pair_layout(bias[:, G_Q]).reshape(depth, d, 1),
        "w_v_t": group(G_V).swapaxes(1, 2).astype(BF16), "b_v": bias[:, G_V].reshape(depth, d, 1),
        "conv_w": jnp.pad(conv_w, ((0, 0), (0, 1), (0, 0))), "conv_b": row(conv_b),
        "cln_g": row(conv_ln_g), "cln_b": row(conv_ln_b),
        "w_pw2": _col_chunks(w_pw2).astype(BF16), "w_proj_a": _col_chunks(w_proj_a).astype(BF16),
        "w_proj_b": w_proj_b.astype(BF16), "w_out": w_out.astype(BF16),
        "ln_g": row(ln_g), "ln_b": row(ln_b),
    }


def kernel(x, w_in, b_in, conv_w, conv_b, conv_ln_g, conv_ln_b, w_pw2, w_proj_a, w_proj_b, w_out, ln_g, ln_b):
    batch, seq_len, d = x.shape
    depth = w_in.shape[0]
    assert d == N_HEADS * HEAD_DIM and w_in.shape[2] == N_GROUPS * d
    assert seq_len % MOBA_BLOCK == 0 and seq_len // MOBA_BLOCK <= LANES
    assert seq_len % CONV_ROW_TILE == 0 and conv_w.shape[1] == CONV_KERNEL
    alpha = (2 * depth) ** 0.25
    tables = _rope_tables(seq_len)
    xf = x.reshape(batch * seq_len, d)
    w = _prepare_params(w_in, b_in, conv_w, conv_b, conv_ln_g, conv_ln_b, w_pw2, w_proj_a, w_proj_b, w_out,
                        ln_g, ln_b)
    for l in range(depth):
        ya = _conv_branch(xf, w, l, seq_len)
        k, km, qt, vt = _attn_proj(xf, w, l, tables, batch, seq_len)
        ot = _moba_attention(k, km, qt, vt)
        xf = _merge(xf, ot, ya, w, l, alpha)
    return xf.reshape(batch, seq_len, d)
```

```python
import functools
import math

import jax
import jax.numpy as jnp
import numpy as np
from jax import lax
from jax.experimental import pallas as pl
from jax.experimental.pallas import tpu as pltpu

N_HEADS = 16
HEAD_DIM = 64
HALF = HEAD_DIM // 2
MOBA_BLOCK = 256
MOBA_TOP_K = 3
CONV_KERNEL = 31
ROPE_THETA = 10000.0
LN_EPS = 1e-5
NEG_INF = -1e30
LOG2_E = math.log2(math.e)
G_VAL, G_GLU, G_AZ, G_Q, G_K, G_V, G_BZ, G_GA, G_GB = range(9)
N_GROUPS = 9

LANES = 128
SUBLANES = 8
HEADS_PER_STEP = LANES // HEAD_DIM
CONV_HALO = 32
CONV_ROWS = 64
CONV_SUB_ROWS = 256
CONV_ROW_TILE = 512
MXU_COLS = 256
ATTN_HEADS_PER_STEP = 16
PROJ_BLOCKS_PER_STEP = 4
MERGE_BLOCKS_PER_STEP = 4
ONES_ROWS = 16
ROW_TILE = MOBA_BLOCK
VMEM_LIMIT = 56 * 1024 * 1024

BF16 = jnp.bfloat16
F32 = jnp.float32


def _dot(a, b):
    return jnp.dot(a, b, preferred_element_type=F32)


def _dot_nt(a, b):
    return lax.dot_general(a, b, (((1,), (1,)), ((), ())), preferred_element_type=F32)


def _sigmoid(v):
    return 1.0 / (1.0 + jnp.exp(-v))


def _silu(v):
    return v * _sigmoid(v)


def _layer_norm(v, g, b):
    mu = jnp.mean(v, axis=-1, keepdims=True)
    vc = v - mu
    var = jnp.mean(vc * vc, axis=-1, keepdims=True)
    return vc * lax.rsqrt(var + LN_EPS) * g + b


def _conv_branch_kernel(x_ref, wval_ref, wglu_ref, wz_ref, wga_ref, bin_ref,
                        cw_ref, cb_ref, cg_ref, cbeta_ref, wpw2_ref, wpa_ref, out_ref,
                        hbuf, cbuf, za_buf, ga_buf, y2_buf, *, tiles_per_batch):
    tm, d = x_ref.shape
    n_sub = tm // CONV_SUB_ROWS
    n_chunks = d // LANES
    n_cols = d // MXU_COLS
    assert n_chunks == 2 * n_cols
    t = pl.program_id(0)
    bval_ref, bglu_ref, bz_ref, bga_ref = (bin_ref.at[g:g + 1] for g in (G_VAL, G_GLU, G_AZ, G_GA))

    @pl.when(t % tiles_per_batch == 0)
    def _():
        hbuf[:, 0:CONV_HALO, :] = jnp.zeros((n_chunks, CONV_HALO, LANES), F32)

    xb = [x_ref[u * CONV_SUB_ROWS:(u + 1) * CONV_SUB_ROWS, :].astype(BF16) for u in range(n_sub)]

    def glu_to_hbuf(u):
        for n in range(n_cols):
            cols = slice(n * MXU_COLS, (n + 1) * MXU_COLS)
            a_val = _dot(xb[u], wval_ref[n]) + bval_ref[:, cols]
            a_glu = _dot(xb[u], wglu_ref[n]) + bglu_ref[:, cols]
            h = a_val * _sigmoid(a_glu)
            for c2 in range(MXU_COLS // LANES):
                rows = slice(CONV_HALO + u * CONV_SUB_ROWS, CONV_HALO + (u + 1) * CONV_SUB_ROWS)
                hbuf[n * (MXU_COLS // LANES) + c2, rows, :] = h[:, c2 * LANES:(c2 + 1) * LANES]

    def conv_chunk(u, c):
        first = CONV_HALO - (CONV_KERNEL - 1)
        lanes = slice(c * LANES, (c + 1) * LANES)
        for r in range(CONV_SUB_ROWS // CONV_ROWS):
            base = u * CONV_SUB_ROWS + r * CONV_ROWS + first
            acc = jnp.broadcast_to(cb_ref[:, lanes], (CONV_ROWS, LANES))
            for k in range(CONV_KERNEL):
                acc = acc + hbuf[c, base + k:base + k + CONV_ROWS, :] * cw_ref[k:k + 1, lanes]
            cbuf[u, r * CONV_ROWS:(r + 1) * CONV_ROWS, lanes] = acc

    def gate_piece(u, n):
        if n < n_cols:
            cols = slice(n * MXU_COLS, (n + 1) * MXU_COLS)
            za_buf[u, :, cols] = _silu(_dot(xb[u], wz_ref[n % n_cols]) + bz_ref[:, cols])
        else:
            cols = slice((n - n_cols) * MXU_COLS, (n - n_cols + 1) * MXU_COLS)
            ga_buf[u, :, cols] = _sigmoid(_dot(xb[u], wga_ref[n % n_cols]) + bga_ref[:, cols])

    def normed(u):
        return _silu(_layer_norm(cbuf[u], cg_ref[...], cbeta_ref[...])).astype(BF16)

    def tail_piece(u, y, n):
        if n < n_cols:
            cols = slice(n * MXU_COLS, (n + 1) * MXU_COLS)
            y2_buf[u, :, cols] = (_dot(y, wpw2_ref[n % n_cols]) * za_buf[u, :, cols]).astype(BF16)
        else:
            cols = slice((n - n_cols) * MXU_COLS, (n - n_cols + 1) * MXU_COLS)
            rows = slice(u * CONV_SUB_ROWS, (u + 1) * CONV_SUB_ROWS)
            out_ref[rows, cols] = _dot(y2_buf[u], wpa_ref[n % n_cols]) * ga_buf[u, :, cols]

    for u in range(n_sub):
        glu_to_hbuf(u)
    y_prev = None
    for u in range(n_sub):
        for c in range(n_chunks):
            conv_chunk(u, c)
            gate_piece(u, c)
            if y_prev is not None:
                tail_piece(u - 1, y_prev, c)
        y_prev = normed(u)
    for n in range(n_chunks):
        tail_piece(n_sub - 1, y_prev, n)

    for c in range(n_chunks):
        hbuf[c, 0:CONV_HALO, :] = hbuf[c, tm:tm + CONV_HALO, :]


def _col_chunks(w):
    k, n = w.shape[-2:]
    return jnp.swapaxes(w.reshape(w.shape[:-1] + (n // MXU_COLS, MXU_COLS)), -3, -2)


def _group_spec(l, g, d):
    n = d // MXU_COLS
    return pl.BlockSpec((None, n, d, MXU_COLS), lambda *_: (l, g, 0, 0), pipeline_mode=pl.Buffered(1))


def _layer_spec(l, shape):
    return pl.BlockSpec((None,) + shape, lambda *_: (l,) + (0,) * len(shape), pipeline_mode=pl.Buffered(1))


def _conv_branch(xf, w, l, seq_len):
    m, d = xf.shape
    tm = CONV_ROW_TILE
    n_sub = tm // CONV_SUB_ROWS
    chunked = (d // MXU_COLS, d, MXU_COLS)
    row = (1, d)
    return pl.pallas_call(
        functools.partial(_conv_branch_kernel, tiles_per_batch=seq_len // tm),
        grid=(m // tm,),
        in_specs=[pl.BlockSpec((tm, d), lambda t: (t, 0)),
                  _group_spec(l, G_VAL, d), _group_spec(l, G_GLU, d), _group_spec(l, G_AZ, d),
                  _group_spec(l, G_GA, d), _layer_spec(l, (N_GROUPS, d)),
                  _layer_spec(l, w["conv_w"].shape[1:]), _layer_spec(l, row), _layer_spec(l, row),
                  _layer_spec(l, row), _layer_spec(l, chunked), _layer_spec(l, chunked)],
        out_specs=pl.BlockSpec((tm, d), lambda t: (t, 0)),
        out_shape=jax.ShapeDtypeStruct((m, d), F32),
        scratch_shapes=[pltpu.VMEM((d // LANES, tm + CONV_HALO, LANES), F32),
                        pltpu.VMEM((n_sub, CONV_SUB_ROWS, d), F32),
                        pltpu.VMEM((n_sub, CONV_SUB_ROWS, d), F32),
                        pltpu.VMEM((n_sub, CONV_SUB_ROWS, d), F32),
                        pltpu.VMEM((n_sub, CONV_SUB_ROWS, d), BF16)],
        compiler_params=pltpu.CompilerParams(dimension_semantics=("arbitrary",),
                                             vmem_limit_bytes=VMEM_LIMIT),
        name="conv_branch",
    )(xf, w["w_in"], w["w_in"], w["w_in"], w["w_in"], w["b_in"],
      w["conv_w"], w["conv_b"], w["cln_g"], w["cln_b"], w["w_pw2"], w["w_proj_a"])


def _attn_proj_kernel(x_ref, wk_ref, bk_ref, wqt_ref, bq_ref, wvt_ref, bv_ref,
                      cosk_ref, sink_ref, cosq_ref, sinq_ref, k_ref, km_ref, qt_ref, vt_ref):
    d = x_ref.shape[1]
    tb = k_ref.shape[1]
    scale = LOG2_E / math.sqrt(HEAD_DIM)

    for u in range(k_ref.shape[0]):
        tok = slice(u * tb, (u + 1) * tb)
        xb = x_ref[tok, :].astype(BF16)

        k = jnp.concatenate([_dot(xb, wk_ref[n]) for n in range(wk_ref.shape[0])], axis=1) + bk_ref[...]
        cos_k = cosk_ref[tok, :]
        sin_k = sink_ref[tok, :]
        for c in range(d // LANES):
            lanes = slice(c * LANES, (c + 1) * LANES)
            k_rot = k[:, lanes] * cos_k + pltpu.roll(k[:, lanes], LANES // 2, axis=1) * sin_k
            k_ref[u, :, lanes] = k_rot.astype(BF16)
            km_ref[u, :, lanes] = jnp.broadcast_to(jnp.mean(k_rot, axis=0, keepdims=True), (SUBLANES, LANES))

        qt = _dot_nt(wqt_ref[...], xb) + bq_ref[...]
        cos_q = cosq_ref[:, tok]
        sin_q = sinq_ref[:, tok]
        for c in range(d // LANES):
            rows = slice(c * LANES, (c + 1) * LANES)
            q = qt[rows]
            partner = jnp.concatenate([q[LANES // 2:], q[:LANES // 2]], axis=0)
            q_rot = (q * cos_q + partner * sin_q) * scale
            qt_ref[u, rows, :] = q_rot.astype(BF16)

        vt = _dot_nt(wvt_ref[...], xb) + bv_ref[...]
        vt_ref[u] = vt.astype(BF16)


def _attn_proj(xf, w, l, tables, batch, seq_len):
    m, d = xf.shape
    tb = MOBA_BLOCK
    nb = seq_len // tb
    per_step = PROJ_BLOCKS_PER_STEP
    assert nb % per_step == 0
    tm = per_step * tb
    steps = nb // per_step
    w_spec = _layer_spec(l, (d, d))
    row_spec = _layer_spec(l, (1, d))
    col_spec = _layer_spec(l, (d, 1))
    cos_k, sin_k, cos_q, sin_q = tables
    blocks = lambda *shape: pl.BlockSpec((None, per_step) + shape, lambda t: (t // steps, t % steps, 0, 0))
    k, km, qt, vt = pl.pallas_call(
        _attn_proj_kernel,
        grid=(m // tm,),
        in_specs=[pl.BlockSpec((tm, d), lambda t: (t, 0)),
                  _layer_spec(l, (d // MXU_COLS, d, MXU_COLS)), row_spec, w_spec, col_spec, w_spec, col_spec,
                  pl.BlockSpec((tm, LANES), lambda t: (t % steps, 0)),
                  pl.BlockSpec((tm, LANES), lambda t: (t % steps, 0)),
                  pl.BlockSpec((LANES, tm), lambda t: (0, t % steps)),
                  pl.BlockSpec((LANES, tm), lambda t: (0, t % steps))],
        out_specs=[blocks(tb, d), blocks(SUBLANES, d), blocks(d, tb), blocks(d, tb)],
        out_shape=[jax.ShapeDtypeStruct((batch, nb, tb, d), BF16),
                   jax.ShapeDtypeStruct((batch, nb, SUBLANES, d), F32),
                   jax.ShapeDtypeStruct((batch, nb, d, tb), BF16),
                   jax.ShapeDtypeStruct((batch, nb, d, tb), BF16)],
        compiler_params=pltpu.CompilerParams(dimension_semantics=("arbitrary",),
                                             vmem_limit_bytes=VMEM_LIMIT),
        name="attn_proj",
    )(xf, w["w_k"], w["b_k"], w["w_q_t"], w["b_q"], w["w_v_t"], w["b_v"],
      cos_k, sin_k, cos_q, sin_q)
    return k, km[:, :, 0, :], qt, vt


def _select_bias(gate, own, nb):
    blk = lax.broadcasted_iota(jnp.int32, gate.shape, 0)
    past = blk < own
    g = jnp.where(past, gate, NEG_INF)
    sel = jnp.zeros(gate.shape, jnp.bool_)
    for _ in range(min(MOBA_TOP_K, nb)):
        best = jnp.max(g, axis=0, keepdims=True)
        first = jnp.min(jnp.where(g == best, blk, nb), axis=0, keepdims=True)
        pick = blk == first
        sel = jnp.logical_or(sel, pick)
        g = jnp.where(pick, -jnp.inf, g)
    return jnp.where(jnp.logical_and(sel, past), 0.0, NEG_INF)


def _moba_kernel(qt_ref, k_ref, vt_ref, km_ref, ot_ref, qh_scr, bias_scr, m_scr, acc_scr, s_even, s_odd,
                 smax_even, smax_odd):
    nb = k_ref.shape[0]
    tk = k_ref.shape[1]
    tq = qt_ref.shape[1]
    n_heads = qt_ref.shape[0] // HEAD_DIM
    own = pl.program_id(2)

    feat_row = lax.broadcasted_iota(jnp.int32, (LANES, tq), 0)
    feat_lane = lax.broadcasted_iota(jnp.int32, (nb, LANES), 1)
    key_pos = lax.broadcasted_iota(jnp.int32, (tk, tq), 0)
    qry_pos = lax.broadcasted_iota(jnp.int32, (tk, tq), 1)
    ones = jnp.ones((ONES_ROWS, tk), BF16)

    def pair_of(h):
        return slice((h // HEADS_PER_STEP) * LANES, (h // HEADS_PER_STEP + 1) * LANES)

    def v_aug(j, h):
        return jnp.concatenate([vt_ref[j, h * HEAD_DIM:(h + 1) * HEAD_DIM, :], ones], axis=0)

    def past_scores(j, h):
        return _dot(k_ref[j, :, pair_of(h)], qh_scr[h])

    def stash(s_ref, smax_ref, h, s):
        s_ref[h] = s
        smax_ref[h] = jnp.broadcast_to(jnp.max(s, axis=0, keepdims=True), (SUBLANES, tq))

    def absorb(j, h, s_ref, smax_ref, bias):
        m = m_scr[h]
        m_new = jnp.maximum(m, smax_ref[h] + bias)
        p = jnp.exp2((s_ref[h] - (m_new - bias)[0:1]).astype(BF16))
        acc_scr[h] = acc_scr[h] * jnp.exp2(m - m_new)[0:1] + _dot(v_aug(j, h), p)
        m_scr[h] = m_new

    for h in range(n_heads):
        q2 = qt_ref[pair_of(h), :]
        km = km_ref[:, pair_of(h)]
        qh = jnp.where((feat_row // HALF) % HEADS_PER_STEP == h % HEADS_PER_STEP, q2, jnp.zeros_like(q2))
        kmh = jnp.where((feat_lane // HALF) % HEADS_PER_STEP == h % HEADS_PER_STEP, km, 0.0).astype(BF16)
        gate = _dot(kmh, q2)
        bias = _select_bias(gate, own, nb)
        bias_scr[h] = bias
        qh_scr[h] = qh
        s_own = _dot(k_ref[own, :, pair_of(h)], qh)
        stash(s_odd, smax_odd, h, jnp.where(key_pos <= qry_pos, s_own, NEG_INF))
        m_scr[h] = jnp.full((SUBLANES, tq), NEG_INF, F32)
        acc_scr[h] = jnp.zeros(acc_scr.shape[1:], F32)

    for h in range(n_heads):
        stash(s_even, smax_even, h, past_scores(0, h))
        absorb(own, h, s_odd, smax_odd, jnp.zeros((1, tq), F32))

    def two_blocks(t, carry):
        j0 = 2 * t
        j1 = jnp.minimum(j0 + 1, nb - 1)
        j2 = jnp.minimum(j0 + 2, nb - 1)
        for h in range(n_heads):
            stash(s_odd, smax_odd, h, past_scores(j1, h))
            absorb(j0, h, s_even, smax_even, bias_scr[h, pl.ds(j0, 1), :])
        for h in range(n_heads):
            stash(s_even, smax_even, h, past_scores(j2, h))
            absorb(j1, h, s_odd, smax_odd, bias_scr[h, pl.ds(j1, 1), :])
        return carry

    lax.fori_loop(0, (own + 1) // 2, two_blocks, 0)
    for h in range(n_heads):
        ot_ref[h * HEAD_DIM:(h + 1) * HEAD_DIM, :] = (acc_scr[h, 0:HEAD_DIM, :]
                                                       / acc_scr[h, HEAD_DIM:HEAD_DIM + 1, :])


def _moba_attention(k, km, qt, vt):
    batch, nb, tk, d = k.shape
    tq = qt.shape[3]
    width = ATTN_HEADS_PER_STEP * HEAD_DIM
    resident = pl.Buffered(1)
    return pl.pallas_call(
        _moba_kernel,
        grid=(batch, d // width, nb),
        in_specs=[pl.BlockSpec((None, None, width, tq), lambda b, g, i: (b, i, g, 0)),
                  pl.BlockSpec((None, nb, tk, width), lambda b, g, i: (b, 0, 0, g), pipeline_mode=resident),
                  pl.BlockSpec((None, nb, width, tk), lambda b, g, i: (b, 0, g, 0), pipeline_mode=resident),
                  pl.BlockSpec((None, nb, width), lambda b, g, i: (b, 0, g))],
        out_specs=pl.BlockSpec((None, None, width, tq), lambda b, g, i: (b, i, g, 0)),
        out_shape=jax.ShapeDtypeStruct((batch, nb, d, tq), F32),
        scratch_shapes=[pltpu.VMEM((ATTN_HEADS_PER_STEP, LANES, tq), BF16),
                        pltpu.VMEM((ATTN_HEADS_PER_STEP, nb, tq), F32),
                        pltpu.VMEM((ATTN_HEADS_PER_STEP, SUBLANES, tq), F32),
                        pltpu.VMEM((ATTN_HEADS_PER_STEP, HEAD_DIM + ONES_ROWS, tq), F32),
                        pltpu.VMEM((ATTN_HEADS_PER_STEP, tk, tq), F32),
                        pltpu.VMEM((ATTN_HEADS_PER_STEP, tk, tq), F32),
                        pltpu.VMEM((ATTN_HEADS_PER_STEP, SUBLANES, tq), F32),
                        pltpu.VMEM((ATTN_HEADS_PER_STEP, SUBLANES, tq), F32)],
        compiler_params=pltpu.CompilerParams(dimension_semantics=("arbitrary", "arbitrary", "arbitrary"),
                                             vmem_limit_bytes=VMEM_LIMIT),
        name="moba_attn",
    )(qt, k, vt, km)


def _merge_kernel(x_ref, ot_ref, ya_ref, wbz_ref, wgb_ref, bin_ref, wpb_ref, wout_ref,
                  g_ref, beta_ref, out_ref, *, alpha):
    n_cols = wbz_ref.shape[0]
    tb = ot_ref.shape[2]
    for u in range(ot_ref.shape[0]):
        rows = slice(u * tb, (u + 1) * tb)
        x = x_ref[rows, :]
        xb = x.astype(BF16)
        o = ot_ref[u].T
        b_z = jnp.concatenate([_dot(xb, wbz_ref[n]) for n in range(n_cols)], axis=1) + bin_ref[G_BZ:G_BZ + 1, :]
        y_b = _dot((o * _silu(b_z)).astype(BF16), wpb_ref[...])
        g_b = jnp.concatenate([_dot(xb, wgb_ref[n]) for n in range(n_cols)], axis=1) + bin_ref[G_GB:G_GB + 1, :]
        merged = ya_ref[rows, :] + _sigmoid(g_b) * y_b
        out = _dot(merged.astype(BF16), wout_ref[...])
        out_ref[rows, :] = _layer_norm(alpha * x + out, g_ref[...], beta_ref[...])


def _merge(xf, ot, ya, w, l, alpha):
    m, d = xf.shape
    batch, nb, _, tb = ot.shape
    per_step = MERGE_BLOCKS_PER_STEP
    assert nb % per_step == 0
    tm = per_step * tb
    steps = nb // per_step
    w_spec = _layer_spec(l, (d, d))
    b_spec = _layer_spec(l, (1, d))
    return pl.pallas_call(
        functools.partial(_merge_kernel, alpha=alpha),
        grid=(m // tm,),
        in_specs=[pl.BlockSpec((tm, d), lambda t: (t, 0)),
                  pl.BlockSpec((None, per_step, d, tb), lambda t: (t // steps, t % steps, 0, 0)),
                  pl.BlockSpec((tm, d), lambda t: (t, 0)),
                  _group_spec(l, G_BZ, d), _group_spec(l, G_GB, d), _layer_spec(l, (N_GROUPS, d)),
                  w_spec, w_spec, b_spec, b_spec],
        out_specs=pl.BlockSpec((tm, d), lambda t: (t, 0)),
        out_shape=jax.ShapeDtypeStruct((m, d), F32),
        compiler_params=pltpu.CompilerParams(dimension_semantics=("arbitrary",),
                                             vmem_limit_bytes=VMEM_LIMIT),
        name="merge",
    )(xf, ot, ya, w["w_in"], w["w_in"], w["b_in"], w["w_proj_b"], w["w_out"], w["ln_g"], w["ln_b"])


def _rope_tables(seq_len):
    inv_freq = np.float32(ROPE_THETA) ** (-np.arange(HALF, dtype=np.float32) / np.float32(HALF))
    ang = np.arange(seq_len, dtype=np.float32)[:, None] * inv_freq[None, :].astype(np.float32)
    cos, sin = jnp.asarray(np.cos(ang), F32), jnp.asarray(np.sin(ang), F32)
    cos_k = jnp.tile(cos, (1, 2 * HEADS_PER_STEP))
    sin_k = jnp.concatenate([-sin] * HEADS_PER_STEP + [sin] * HEADS_PER_STEP, axis=1)
    return cos_k, sin_k, cos_k.T, sin_k.T


def _prepare_params(w_in, b_in, conv_w, conv_b, conv_ln_g, conv_ln_b, w_pw2, w_proj_a, w_proj_b, w_out,
                    ln_g, ln_b):
    depth, d, _ = w_in.shape
    group = lambda g: w_in[:, :, g * d:(g + 1) * d]
    bias = b_in.reshape(depth, N_GROUPS, d)
    pair_layout = lambda v: (v.reshape(v.shape[:-1] + (N_HEADS // HEADS_PER_STEP, HEADS_PER_STEP, 2, HALF))
                             .swapaxes(-3, -2).reshape(v.shape))
    row = lambda v: v.reshape(depth, 1, d)
    return {
        "w_in": _col_chunks(w_in).astype(BF16),
        "b_in": bias,
        "w_k": _col_chunks(pair_layout(group(G_K))).astype(BF16), "b_k": row(pair_layout(bias[:, G_K])),
        "w_q_t": pair_layout(group(G_Q)).swapaxes(1, 2).astype(BF16),
        "b_q": pair_layout(bias[:, G_Q]).reshape(depth, d, 1),
        "w_v_t": group(G_V).swapaxes(1, 2).astype(BF16), "b_v": bias[:, G_V].reshape(depth, d, 1),
        "conv_w": jnp.pad(conv_w, ((0, 0), (0, 1), (0, 0))), "conv_b": row(conv_b),
        "cln_g": row(conv_ln_g), "cln_b": row(conv_ln_b),
        "w_pw2": _col_chunks(w_pw2).astype(BF16), "w_proj_a": _col_chunks(w_proj_a).astype(BF16),
        "w_proj_b": w_proj_b.astype(BF16), "w_out": w_out.astype(BF16),
        "ln_g": row(ln_g), "ln_b": row(ln_b),
    }


def kernel(x, w_in, b_in, conv_w, conv_b, conv_ln_g, conv_ln_b, w_pw2, w_proj_a, w_proj_b, w_out, ln_g, ln_b):
    batch, seq_len, d = x.shape
    depth = w_in.shape[0]
    assert d == N_HEADS * HEAD_DIM and w_in.shape[2] == N_GROUPS * d
    assert seq_len % MOBA_BLOCK == 0 and seq_len // MOBA_BLOCK <= LANES
    assert seq_len % CONV_ROW_TILE == 0 and conv_w.shape[1] == CONV_KERNEL
    alpha = (2 * depth) ** 0.25
    tables = _rope_tables(seq_len)
    xf = x.reshape(batch * seq_len, d)
    w = _prepare_params(w_in, b_in, conv_w, conv_b, conv_ln_g, conv_ln_b, w_pw2, w_proj_a, w_proj_b, w_out,
                        ln_g, ln_b)
    for l in range(depth):
        ya = _conv_branch(xf, w, l, seq_len)
        k, km, qt, vt = _attn_proj(xf, w, l, tables, batch, seq_len)
        ot = _moba_attention(k, km, qt, vt)
        xf = _merge(xf, ot, ya, w, l, alpha)
    return xf.reshape(batch, seq_len, d)
```

```python
import functools
import math

import jax
import jax.numpy as jnp
import numpy as np
from jax import lax
from jax.experimental import pallas as pl
from jax.experimental.pallas import tpu as pltpu

N_HEADS = 16
HEAD_DIM = 64
HALF = HEAD_DIM // 2
MOBA_BLOCK = 256
MOBA_TOP_K = 3
CONV_KERNEL = 31
ROPE_THETA = 10000.0
LN_EPS = 1e-5
NEG_INF = -1e30
LOG2_E = math.log2(math.e)
G_VAL, G_GLU, G_AZ, G_Q, G_K, G_V, G_BZ, G_GA, G_GB = range(9)
N_GROUPS = 9

LANES = 128
SUBLANES = 8
HEADS_PER_STEP = LANES // HEAD_DIM
CONV_HALO = 32
CONV_ROWS = 64
MXU_COLS = 256
ATTN_HEADS_PER_STEP = 16
PROJ_BLOCKS_PER_STEP = 4
MERGE_BLOCKS_PER_STEP = 4
ONES_ROWS = 16
ROW_TILE = MOBA_BLOCK
VMEM_LIMIT = 56 * 1024 * 1024

BF16 = jnp.bfloat16
F32 = jnp.float32


def _dot(a, b):
    return jnp.dot(a, b, preferred_element_type=F32)


def _dot_nt(a, b):
    return lax.dot_general(a, b, (((1,), (1,)), ((), ())), preferred_element_type=F32)


def _sigmoid(v):
    return 1.0 / (1.0 + jnp.exp(-v))


def _silu(v):
    return v * _sigmoid(v)


def _layer_norm(v, g, b):
    mu = jnp.mean(v, axis=-1, keepdims=True)
    vc = v - mu
    var = jnp.mean(vc * vc, axis=-1, keepdims=True)
    return vc * lax.rsqrt(var + LN_EPS) * g + b


def _conv_branch_kernel(x_ref, xprev_ref, wval_ref, wglu_ref, wz_ref, wga_ref, bin_ref,
                        cw_ref, cb_ref, cg_ref, cbeta_ref, wpw2_ref, wpa_ref, out_ref,
                        hbuf, cbuf, *, tiles_per_batch):
    tm, d = x_ref.shape
    n_chunks = d // LANES
    n_cols = d // MXU_COLS
    t = pl.program_id(0)
    bval_ref, bglu_ref, bz_ref, bga_ref = (bin_ref.at[g:g + 1] for g in (G_VAL, G_GLU, G_AZ, G_GA))

    @pl.when(t == 0)
    def _():
        cbuf[...] = jnp.zeros(cbuf.shape, F32)

    @pl.when(t % tiles_per_batch == 0)
    def _():
        hbuf[:, 0:CONV_HALO, :] = jnp.zeros((n_chunks, CONV_HALO, LANES), F32)

    def cols_of(n):
        return slice(n * MXU_COLS, (n + 1) * MXU_COLS)

    xb = x_ref[...].astype(BF16)
    for n in range(n_cols):
        a_val = _dot(xb, wval_ref[n]) + bval_ref[:, cols_of(n)]
        a_glu = _dot(xb, wglu_ref[n]) + bglu_ref[:, cols_of(n)]
        h = a_val * _sigmoid(a_glu)
        for c2 in range(MXU_COLS // LANES):
            hbuf[n * (MXU_COLS // LANES) + c2, CONV_HALO:CONV_HALO + tm, :] = h[:, c2 * LANES:(c2 + 1) * LANES]

    xp = xprev_ref[...].astype(BF16)
    y = _silu(_layer_norm(cbuf[...], cg_ref[...], cbeta_ref[...])).astype(BF16)
    y2 = jnp.concatenate(
        [(_dot(y, wpw2_ref[n]) * _silu(_dot(xp, wz_ref[n]) + bz_ref[:, cols_of(n)])).astype(BF16)
         for n in range(n_cols)], axis=1)
    for n in range(n_cols):
        out_ref[:, cols_of(n)] = (_dot(y2, wpa_ref[n])
                                  * _sigmoid(_dot(xp, wga_ref[n]) + bga_ref[:, cols_of(n)]))

    for c in range(n_chunks):
        lanes = slice(c * LANES, (c + 1) * LANES)
        for r in range(tm // CONV_ROWS):
            base = r * CONV_ROWS + CONV_HALO - (CONV_KERNEL - 1)
            acc = jnp.broadcast_to(cb_ref[:, lanes], (CONV_ROWS, LANES))
            for k in range(CONV_KERNEL):
                acc = acc + hbuf[c, base + k:base + k + CONV_ROWS, :] * cw_ref[k:k + 1, lanes]
            cbuf[r * CONV_ROWS:(r + 1) * CONV_ROWS, lanes] = acc

    for c in range(n_chunks):
        hbuf[c, 0:CONV_HALO, :] = hbuf[c, tm:tm + CONV_HALO, :]


def _col_chunks(w):
    k, n = w.shape[-2:]
    return jnp.swapaxes(w.reshape(w.shape[:-1] + (n // MXU_COLS, MXU_COLS)), -3, -2)


def _group_spec(l, g, d):
    n = d // MXU_COLS
    return pl.BlockSpec((None, n, d, MXU_COLS), lambda *_: (l, g, 0, 0), pipeline_mode=pl.Buffered(1))


def _layer_spec(l, shape):
    return pl.BlockSpec((None,) + shape, lambda *_: (l,) + (0,) * len(shape), pipeline_mode=pl.Buffered(1))


def _conv_branch(xf, w, l, seq_len):
    m, d = xf.shape
    tm = ROW_TILE
    n_tiles = m // tm
    chunked = (d // MXU_COLS, d, MXU_COLS)
    row = (1, d)
    return pl.pallas_call(
        functools.partial(_conv_branch_kernel, tiles_per_batch=seq_len // tm),
        grid=(n_tiles + 1,),
        in_specs=[pl.BlockSpec((tm, d), lambda t: (jnp.minimum(t, n_tiles - 1), 0)),
                  pl.BlockSpec((tm, d), lambda t: (jnp.maximum(t - 1, 0), 0)),
                  _group_spec(l, G_VAL, d), _group_spec(l, G_GLU, d), _group_spec(l, G_AZ, d),
                  _group_spec(l, G_GA, d), _layer_spec(l, (N_GROUPS, d)),
                  _layer_spec(l, w["conv_w"].shape[1:]), _layer_spec(l, row), _layer_spec(l, row),
                  _layer_spec(l, row), _layer_spec(l, chunked), _layer_spec(l, chunked)],
        out_specs=pl.BlockSpec((tm, d), lambda t: (jnp.maximum(t - 1, 0), 0)),
        out_shape=jax.ShapeDtypeStruct((m, d), F32),
        scratch_shapes=[pltpu.VMEM((d // LANES, tm + CONV_HALO, LANES), F32),
                        pltpu.VMEM((tm, d), F32)],
        compiler_params=pltpu.CompilerParams(dimension_semantics=("arbitrary",),
                                             vmem_limit_bytes=VMEM_LIMIT),
        name="conv_branch",
    )(xf, xf, w["w_in"], w["w_in"], w["w_in"], w["w_in"], w["b_in"],
      w["conv_w"], w["conv_b"], w["cln_g"], w["cln_b"], w["w_pw2"], w["w_proj_a"])


def _attn_proj_kernel(x_ref, wk_ref, bk_ref, wqt_ref, bq_ref, wvt_ref, bv_ref,
                      cosk_ref, sink_ref, cosq_ref, sinq_ref, k_ref, km_ref, qt_ref, vt_ref):
    d = x_ref.shape[1]
    tb = k_ref.shape[1]
    scale = LOG2_E / math.sqrt(HEAD_DIM)

    for u in range(k_ref.shape[0]):
        tok = slice(u * tb, (u + 1) * tb)
        xb = x_ref[tok, :].astype(BF16)

        k = jnp.concatenate([_dot(xb, wk_ref[n]) for n in range(wk_ref.shape[0])], axis=1) + bk_ref[...]
        cos_k = cosk_ref[tok, :]
        sin_k = sink_ref[tok, :]
        for c in range(d // LANES):
            lanes = slice(c * LANES, (c + 1) * LANES)
            k_rot = k[:, lanes] * cos_k + pltpu.roll(k[:, lanes], LANES // 2, axis=1) * sin_k
            k_ref[u, :, lanes] = k_rot.astype(BF16)
            km_ref[u, :, lanes] = jnp.broadcast_to(jnp.mean(k_rot, axis=0, keepdims=True), (SUBLANES, LANES))

        qt = _dot_nt(wqt_ref[...], xb) + bq_ref[...]
        cos_q = cosq_ref[:, tok]
        sin_q = sinq_ref[:, tok]
        for c in range(d // LANES):
            rows = slice(c * LANES, (c + 1) * LANES)
            q = qt[rows]
            partner = jnp.concatenate([q[LANES // 2:], q[:LANES // 2]], axis=0)
            q_rot = (q * cos_q + partner * sin_q) * scale
            qt_ref[u, rows, :] = q_rot.astype(BF16)

        vt = _dot_nt(wvt_ref[...], xb) + bv_ref[...]
        vt_ref[u] = vt.astype(BF16)


def _attn_proj(xf, w, l, tables, batch, seq_len):
    m, d = xf.shape
    tb = MOBA_BLOCK
    nb = seq_len // tb
    per_step = PROJ_BLOCKS_PER_STEP
    assert nb % per_step == 0
    tm = per_step * tb
    steps = nb // per_step
    w_spec = _layer_spec(l, (d, d))
    row_spec = _layer_spec(l, (1, d))
    col_spec = _layer_spec(l, (d, 1))
    cos_k, sin_k, cos_q, sin_q = tables
    blocks = lambda *shape: pl.BlockSpec((None, per_step) + shape, lambda t: (t // steps, t % steps, 0, 0))
    k, km, qt, vt = pl.pallas_call(
        _attn_proj_kernel,
        grid=(m // tm,),
        in_specs=[pl.BlockSpec((tm, d), lambda t: (t, 0)),
                  _layer_spec(l, (d // MXU_COLS, d, MXU_COLS)), row_spec, w_spec, col_spec, w_spec, col_spec,
                  pl.BlockSpec((tm, LANES), lambda t: (t % steps, 0)),
                  pl.BlockSpec((tm, LANES), lambda t: (t % steps, 0)),
                  pl.BlockSpec((LANES, tm), lambda t: (0, t % steps)),
                  pl.BlockSpec((LANES, tm), lambda t: (0, t % steps))],
        out_specs=[blocks(tb, d), blocks(SUBLANES, d), blocks(d, tb), blocks(d, tb)],
        out_shape=[jax.ShapeDtypeStruct((batch, nb, tb, d), BF16),
                   jax.ShapeDtypeStruct((batch, nb, SUBLANES, d), F32),
                   jax.ShapeDtypeStruct((batch, nb, d, tb), BF16),
                   jax.ShapeDtypeStruct((batch, nb, d, tb), BF16)],
        compiler_params=pltpu.CompilerParams(dimension_semantics=("arbitrary",),
                                             vmem_limit_bytes=VMEM_LIMIT),
        name="attn_proj",
    )(xf, w["w_k"], w["b_k"], w["w_q_t"], w["b_q"], w["w_v_t"], w["b_v"],
      cos_k, sin_k, cos_q, sin_q)
    return k, km[:, :, 0, :], qt, vt


def _select_bias(gate, own, nb):
    blk = lax.broadcasted_iota(jnp.int32, gate.shape, 0)
    past = blk < own
    g = jnp.where(past, gate, NEG_INF)
    sel = jnp.zeros(gate.shape, jnp.bool_)
    for _ in range(min(MOBA_TOP_K, nb)):
        best = jnp.max(g, axis=0, keepdims=True)
        first = jnp.min(jnp.where(g == best, blk, nb), axis=0, keepdims=True)
        pick = blk == first
        sel = jnp.logical_or(sel, pick)
        g = jnp.where(pick, -jnp.inf, g)
    return jnp.where(jnp.logical_and(sel, past), 0.0, NEG_INF)


def _moba_kernel(qt_ref, k_ref, vt_ref, km_ref, ot_ref, qh_scr, bias_scr, m_scr, acc_scr, s_even, s_odd,
                 smax_even, smax_odd):
    nb = k_ref.shape[0]
    tk = k_ref.shape[1]
    tq = qt_ref.shape[1]
    n_heads = qt_ref.shape[0] // HEAD_DIM
    own = pl.program_id(2)

    feat_row = lax.broadcasted_iota(jnp.int32, (LANES, tq), 0)
    feat_lane = lax.broadcasted_iota(jnp.int32, (nb, LANES), 1)
    key_pos = lax.broadcasted_iota(jnp.int32, (tk, tq), 0)
    qry_pos = lax.broadcasted_iota(jnp.int32, (tk, tq), 1)
    ones = jnp.ones((ONES_ROWS, tk), BF16)

    def pair_of(h):
        return slice((h // HEADS_PER_STEP) * LANES, (h // HEADS_PER_STEP + 1) * LANES)

    def v_aug(j, h):
        return jnp.concatenate([vt_ref[j, h * HEAD_DIM:(h + 1) * HEAD_DIM, :], ones], axis=0)

    def past_scores(j, h):
        return _dot(k_ref[j, :, pair_of(h)], qh_scr[h])

    def stash(s_ref, smax_ref, h, s):
        s_ref[h] = s
        smax_ref[h] = jnp.broadcast_to(jnp.max(s, axis=0, keepdims=True), (SUBLANES, tq))

    def absorb(j, h, s_ref, smax_ref, bias):
        m = m_scr[h]
        m_new = jnp.maximum(m, smax_ref[h] + bias)
        p = jnp.exp2((s_ref[h] - (m_new - bias)[0:1]).astype(BF16))
        acc_scr[h] = acc_scr[h] * jnp.exp2(m - m_new)[0:1] + _dot(v_aug(j, h), p)
        m_scr[h] = m_new

    for h in range(n_heads):
        q2 = qt_ref[pair_of(h), :]
        km = km_ref[:, pair_of(h)]
        qh = jnp.where((feat_row // HALF) % HEADS_PER_STEP == h % HEADS_PER_STEP, q2, jnp.zeros_like(q2))
        kmh = jnp.where((feat_lane // HALF) % HEADS_PER_STEP == h % HEADS_PER_STEP, km, 0.0).astype(BF16)
        gate = _dot(kmh, q2)
        bias = _select_bias(gate, own, nb)
        bias_scr[h] = bias
        qh_scr[h] = qh
        s_own = _dot(k_ref[own, :, pair_of(h)], qh)
        stash(s_odd, smax_odd, h, jnp.where(key_pos <= qry_pos, s_own, NEG_INF))
        m_scr[h] = jnp.full((SUBLANES, tq), NEG_INF, F32)
        acc_scr[h] = jnp.zeros(acc_scr.shape[1:], F32)

    for h in range(n_heads):
        stash(s_even, smax_even, h, past_scores(0, h))
        absorb(own, h, s_odd, smax_odd, jnp.zeros((1, tq), F32))

    def two_blocks(t, carry):
        j0 = 2 * t
        j1 = jnp.minimum(j0 + 1, nb - 1)
        j2 = jnp.minimum(j0 + 2, nb - 1)
        for h in range(n_heads):
            stash(s_odd, smax_odd, h, past_scores(j1, h))
            absorb(j0, h, s_even, smax_even, bias_scr[h, pl.ds(j0, 1), :])
        for h in range(n_heads):
            stash(s_even, smax_even, h, past_scores(j2, h))
            absorb(j1, h, s_odd, smax_odd, bias_scr[h, pl.ds(j1, 1), :])
        return carry

    lax.fori_loop(0, (own + 1) // 2, two_blocks, 0)
    for h in range(n_heads):
        ot_ref[h * HEAD_DIM:(h + 1) * HEAD_DIM, :] = (acc_scr[h, 0:HEAD_DIM, :]
                                                       / acc_scr[h, HEAD_DIM:HEAD_DIM + 1, :])


def _moba_attention(k, km, qt, vt):
    batch, nb, tk, d = k.shape
    tq = qt.shape[3]
    width = ATTN_HEADS_PER_STEP * HEAD_DIM
    resident = pl.Buffered(1)
    return pl.pallas_call(
        _moba_kernel,
        grid=(batch, d // width, nb),
        in_specs=[pl.BlockSpec((None, None, width, tq), lambda b, g, i: (b, i, g, 0)),
                  pl.BlockSpec((None, nb, tk, width), lambda b, g, i: (b, 0, 0, g), pipeline_mode=resident),
                  pl.BlockSpec((None, nb, width, tk), lambda b, g, i: (b, 0, g, 0), pipeline_mode=resident),
                  pl.BlockSpec((None, nb, width), lambda b, g, i: (b, 0, g))],
        out_specs=pl.BlockSpec((None, None, width, tq), lambda b, g, i: (b, i, g, 0)),
        out_shape=jax.ShapeDtypeStruct((batch, nb, d, tq), F32),
        scratch_shapes=[pltpu.VMEM((ATTN_HEADS_PER_STEP, LANES, tq), BF16),
                        pltpu.VMEM((ATTN_HEADS_PER_STEP, nb, tq), F32),
                        pltpu.VMEM((ATTN_HEADS_PER_STEP, SUBLANES, tq), F32),
                        pltpu.VMEM((ATTN_HEADS_PER_STEP, HEAD_DIM + ONES_ROWS, tq), F32),
                        pltpu.VMEM((ATTN_HEADS_PER_STEP, tk, tq), F32),
                        pltpu.VMEM((ATTN_HEADS_PER_STEP, tk, tq), F32),
                        pltpu.VMEM((ATTN_HEADS_PER_STEP, SUBLANES, tq), F32),
                        pltpu.VMEM((ATTN_HEADS_PER_STEP, SUBLANES, tq), F32)],
        compiler_params=pltpu.CompilerParams(dimension_semantics=("arbitrary", "arbitrary", "arbitrary"),
                                             vmem_limit_bytes=VMEM_LIMIT),
        name="moba_attn",
    )(qt, k, vt, km)


def _merge_kernel(x_ref, ot_ref, ya_ref, wbz_ref, wgb_ref, bin_ref, wpb_ref, wout_ref,
                  g_ref, beta_ref, out_ref, *, alpha):
    n_cols = wbz_ref.shape[0]
    tb = ot_ref.shape[2]
    for u in range(ot_ref.shape[0]):
        rows = slice(u * tb, (u + 1) * tb)
        x = x_ref[rows, :]
        xb = x.astype(BF16)
        o = ot_ref[u].T
        b_z = jnp.concatenate([_dot(xb, wbz_ref[n]) for n in range(n_cols)], axis=1) + bin_ref[G_BZ:G_BZ + 1, :]
        y_b = _dot((o * _silu(b_z)).astype(BF16), wpb_ref[...])
        g_b = jnp.concatenate([_dot(xb, wgb_ref[n]) for n in range(n_cols)], axis=1) + bin_ref[G_GB:G_GB + 1, :]
        merged = ya_ref[rows, :] + _sigmoid(g_b) * y_b
        out = _dot(merged.astype(BF16), wout_ref[...])
        out_ref[rows, :] = _layer_norm(alpha * x + out, g_ref[...], beta_ref[...])


def _merge(xf, ot, ya, w, l, alpha):
    m, d = xf.shape
    batch, nb, _, tb = ot.shape
    per_step = MERGE_BLOCKS_PER_STEP
    assert nb % per_step == 0
    tm = per_step * tb
    steps = nb // per_step
    w_spec = _layer_spec(l, (d, d))
    b_spec = _layer_spec(l, (1, d))
    return pl.pallas_call(
        functools.partial(_merge_kernel, alpha=alpha),
        grid=(m // tm,),
        in_specs=[pl.BlockSpec((tm, d), lambda t: (t, 0)),
                  pl.BlockSpec((None, per_step, d, tb), lambda t: (t // steps, t % steps, 0, 0)),
                  pl.BlockSpec((tm, d), lambda t: (t, 0)),
                  _group_spec(l, G_BZ, d), _group_spec(l, G_GB, d), _layer_spec(l, (N_GROUPS, d)),
                  w_spec, w_spec, b_spec, b_spec],
        out_specs=pl.BlockSpec((tm, d), lambda t: (t, 0)),
        out_shape=jax.ShapeDtypeStruct((m, d), F32),
        compiler_params=pltpu.CompilerParams(dimension_semantics=("arbitrary",),
                                             vmem_limit_bytes=VMEM_LIMIT),
        name="merge",
    )(xf, ot, ya, w["w_in"], w["w_in"], w["b_in"], w["w_proj_b"], w["w_out"], w["ln_g"], w["ln_b"])


def _rope_tables(seq_len):
    inv_freq = np.float32(ROPE_THETA) ** (-np.arange(HALF, dtype=np.float32) / np.float32(HALF))
    ang = np.arange(seq_len, dtype=np.float32)[:, None] * inv_freq[None, :].astype(np.float32)
    cos, sin = jnp.asarray(np.cos(ang), F32), jnp.asarray(np.sin(ang), F32)
    cos_k = jnp.tile(cos, (1, 2 * HEADS_PER_STEP))
    sin_k = jnp.concatenate([-sin] * HEADS_PER_STEP + [sin] * HEADS_PER_STEP, axis=1)
    return cos_k, sin_k, cos_k.T, sin_k.T


def _prepare_params(w_in, b_in, conv_w, conv_b, conv_ln_g, conv_ln_b, w_pw2, w_proj_a, w_proj_b, w_out,
                    ln_g, ln_b):
    depth, d, _ = w_in.shape
    group = lambda g: w_in[:, :, g * d:(g + 1) * d]
    bias = b_in.reshape(depth, N_GROUPS, d)
    pair_layout = lambda v: (v.reshape(v.shape[:-1] + (N_HEADS // HEADS_PER_STEP, HEADS_PER_STEP, 2, HALF))
                             .swapaxes(-3, -2).reshape(v.shape))
    row = lambda v: v.reshape(depth, 1, d)
    return {
        "w_in": _col_chunks(w_in).astype(BF16),
        "b_in": bias,
        "w_k": _col_chunks(pair_layout(group(G_K))).astype(BF16), "b_k": row(pair_layout(bias[:, G_K])),
        "w_q_t": pair_layout(group(G_Q)).swapaxes(1, 2).astype(BF16),
        "b_q": pair_layout(bias[:, G_Q]).reshape(depth, d, 1),
        "w_v_t": group(G_V).swapaxes(1, 2).astype(BF16), "b_v": bias[:, G_V].reshape(depth, d, 1),
        "conv_w": jnp.pad(conv_w, ((0, 0), (0, 1), (0, 0))), "conv_b": row(conv_b),
        "cln_g": row(conv_ln_g), "cln_b": row(conv_ln_b),
        "w_pw2": _col_chunks(w_pw2).astype(BF16), "w_proj_a": _col_chunks(w_proj_a).astype(BF16),
        "w_proj_b": w_proj_b.astype(BF16), "w_out": w_out.astype(BF16),
        "ln_g": row(ln_g), "ln_b": row(ln_b),
    }


def kernel(x, w_in, b_in, conv_w, conv_b, conv_ln_g, conv_ln_b, w_pw2, w_proj_a, w_proj_b, w_out, ln_g, ln_b):
    batch, seq_len, d = x.shape
    depth = w_in.shape[0]
    assert d == N_HEADS * HEAD_DIM and w_in.shape[2] == N_GROUPS * d
    assert seq_len % MOBA_BLOCK == 0 and seq_len // MOBA_BLOCK <= LANES
    assert conv_w.shape[1] == CONV_KERNEL
    alpha = (2 * depth) ** 0.25
    tables = _rope_tables(seq_len)
    xf = x.reshape(batch * seq_len, d)
    w = _prepare_params(w_in, b_in, conv_w, conv_b, conv_ln_g, conv_ln_b, w_pw2, w_proj_a, w_proj_b, w_out,
                        ln_g, ln_b)
    for l in range(depth):
        ya = _conv_branch(xf, w, l, seq_len)
        k, km, qt, vt = _attn_proj(xf, w, l, tables, batch, seq_len)
        ot = _moba_attention(k, km, qt, vt)
        xf = _merge(xf, ot, ya, w, l, alpha)
    return xf.reshape(batch, seq_len, d)
```
